```python
import math
import jax, jax.numpy as jnp
from jax import lax
import numpy as np

D_MODEL = 2048
BATCH = 4
SEQ = 8192
DEPTH = 4

RET_HEADS = 8
RET_HEAD_DIM = 128
RET_WIDTH = RET_HEADS * RET_HEAD_DIM
RET_CHUNK = 128
RET_DECAY_BASE = 5
MLA_HEADS = 8
MLA_Q_LORA = 512
MLA_KV_LORA = 256
MLA_NOPE = 128
MLA_ROPE = 64
MLA_V = 128
MLA_WIDTH = MLA_HEADS * MLA_V
Q_BLOCK = 128
POOL_WINDOWS = (2, 4, 8, 16)
POOL_GROUP = 256
POOL_WIDTH = len(POOL_WINDOWS) * POOL_GROUP
BRANCH_WIDTH = 1024
N_BRANCH = 3
N_GROUPS = 8
EXPERTS_PER_GROUP = 4
N_EXPERTS = N_GROUPS * EXPERTS_PER_GROUP
EXPERT_HIDDEN = 512
TOP_K = 2
MOE_BLOCK = 128
PLE_DIM = 256

ROPE_BASE = 10000.0
NORM_EPS = 1e-6
IN_SIZES = (RET_WIDTH, RET_WIDTH, RET_WIDTH, RET_WIDTH, MLA_Q_LORA, MLA_KV_LORA, MLA_ROPE, POOL_WIDTH, N_BRANCH * D_MODEL)
IN_WIDTH = 12096

kernel_name = "hybrid_retention_mla_pool_hmoe_encoder"


def rmsnorm(x, g):
    xf = x.astype(jnp.float32)
    y = xf * lax.rsqrt(jnp.mean(xf * xf, axis=-1, keepdims=True) + NORM_EPS)
    return (y * g.astype(jnp.float32)).astype(x.dtype)


def rope_tables(seq, dim):
    inv = 1.0 / (ROPE_BASE ** (jnp.arange(0, dim, 2, dtype=jnp.float32) / dim))
    ang = jnp.arange(seq, dtype=jnp.float32)[:, None] * inv[None, :]
    return jnp.cos(ang), jnp.sin(ang)


def apply_rope(x, cos, sin):
    d2 = x.shape[-1] // 2
    x1, x2 = x[..., :d2], x[..., d2:]
    c = cos[None, :, None, :].astype(x.dtype)
    s = sin[None, :, None, :].astype(x.dtype)
    return jnp.concatenate([x1 * c - x2 * s, x2 * c + x1 * s], axis=-1)


def split_columns(proj):
    parts, start = [], 0
    for size in IN_SIZES:
        parts.append(proj[..., start:start + size])
        start += size
    return parts


def chunk_retention(q, k, v, log_gamma, strict):
    B, H, S, dk = q.shape
    dv = v.shape[-1]
    C = RET_CHUNK
    N = S // C
    qc = q.reshape(B, H, N, C, dk)
    kc = k.reshape(B, H, N, C, dk)
    vc = v.reshape(B, H, N, C, dv)
    idx = jnp.arange(C, dtype=jnp.float32)
    diff = idx[:, None] - idx[None, :]
    mask = (diff > 0) if strict else (diff >= 0)
    lg = log_gamma[:, None, None]
    decay_in = jnp.where(mask[None], jnp.exp(lg * jnp.where(mask, diff, 0.0)[None]), 0.0)
    s = jnp.einsum('bhncd,bhnld->bhncl', qc, kc).astype(jnp.float32) * decay_in[None, :, None]
    inner = jnp.einsum('bhncl,bhnlv->bhncv', s.astype(v.dtype), vc)
    zeta = jnp.exp(log_gamma[:, None] * (C - 1 - idx)[None, :])
    kv = jnp.einsum('bhnld,bhnlv->nbhdv', kc.astype(jnp.float32) * zeta[None, :, None, :, None],
                    vc.astype(jnp.float32))
    chunk_decay = jnp.exp(log_gamma * C)[None, :, None, None]

    def step(state, kv_n):
        return state * chunk_decay + kv_n, state

    _, states = lax.scan(step, jnp.zeros((B, H, dk, dv), jnp.float32), kv)
    xi = jnp.exp(log_gamma[:, None] * (idx + 1.0)[None, :])
    cross = jnp.einsum('bhncd,nbhdv->bhncv', qc.astype(jnp.float32), states) * xi[None, :, None, :, None]
    return (inner + cross.astype(v.dtype)).reshape(B, H, S, dv)


def retention_branch(q_raw, k_raw, v_raw, g_raw, decay_logit, norm_gain, cos, sin):
    B, S, _ = q_raw.shape
    shp = (B, S, RET_HEADS, RET_HEAD_DIM)
    q = apply_rope(q_raw.reshape(shp), cos, sin)
    k = apply_rope(k_raw.reshape(shp), cos, sin) * (RET_HEAD_DIM ** -0.5)
    v = v_raw.reshape(shp)
    q, k, v = (a.transpose(0, 2, 1, 3) for a in (q, k, v))
    log_gamma = jax.nn.log_sigmoid(decay_logit.astype(jnp.float32))
    fwd = chunk_retention(q, k, v, log_gamma[0], strict=False)
    bwd = chunk_retention(q[:, :, ::-1], k[:, :, ::-1], v[:, :, ::-1], log_gamma[1], strict=True)[:, :, ::-1]
    y = (fwd + bwd).transpose(0, 2, 1, 3)
    y = rmsnorm(y, norm_gain)
    return y.reshape(B, S, RET_WIDTH) * jax.nn.silu(g_raw)


def block_attention(q, k, v):
    B, H, S, dq = q.shape
    dv = v.shape[-1]
    nb = S // Q_BLOCK
    qb = q.reshape(B, H, nb, Q_BLOCK, dq).transpose(2, 0, 1, 3, 4)

    def one_block(q_blk):
        s = jnp.einsum('bhqd,bhkd->bhqk', q_blk, k).astype(jnp.float32)
        w = jax.nn.softmax(s, axis=-1).astype(v.dtype)
        return jnp.einsum('bhqk,bhkd->bhqd', w, v)

    o = lax.map(one_block, qb)
    return o.transpose(1, 0, 3, 2, 4).reshape(B, S, H * dv)


def mla_branch(cq_raw, ckv_raw, kr_raw, q_norm, w_uq, kv_norm, w_ukv, cos, sin):
    B, S, _ = cq_raw.shape
    q = (rmsnorm(cq_raw, q_norm) @ w_uq).reshape(B, S, MLA_HEADS, MLA_NOPE + MLA_ROPE)
    q = jnp.concatenate([q[..., :MLA_NOPE], apply_rope(q[..., MLA_NOPE:], cos, sin)], axis=-1)
    kv = (rmsnorm(ckv_raw, kv_norm) @ w_ukv).reshape(B, S, MLA_HEADS, MLA_NOPE + MLA_V)
    k_rope = apply_rope(kr_raw[:, :, None, :], cos, sin)
    k = jnp.concatenate([kv[..., :MLA_NOPE], jnp.broadcast_to(k_rope, (B, S, MLA_HEADS, MLA_ROPE))], axis=-1)
    v = kv[..., MLA_NOPE:]
    q = q * ((MLA_NOPE + MLA_ROPE) ** -0.5)
    return block_attention(q.transpose(0, 2, 1, 3), k.transpose(0, 2, 1, 3), v.transpose(0, 2, 1, 3))


def pool_branch(u, pool_w, pool_scale):
    B, S, W = u.shape
    uf = u.astype(jnp.float32)
    cs = jnp.concatenate([jnp.zeros((B, 1, W), jnp.float32), lax.cumsum(uf, axis=1)], axis=1)
    pos = jnp.arange(S, dtype=jnp.int32)
    outs = []
    for gi, w in enumerate(POOL_WINDOWS):
        sl = slice(gi * POOL_GROUP, (gi + 1) * POOL_GROUP)
        lo = jnp.clip(pos - w // 2, 0, S)
        hi = jnp.clip(pos + w // 2, 0, S)
        cg = cs[:, :, sl]
        total = jnp.take(cg, hi, axis=1) - jnp.take(cg, lo, axis=1)
        cnt = (hi - lo).astype(jnp.float32)[None, :, None]
        mixed = (total / cnt - uf[:, :, sl]).astype(u.dtype)
        outs.append(jnp.einsum('bsc,cd->bsd', mixed, pool_w[gi]))
    return jnp.concatenate(outs, axis=-1) * pool_scale


def grouped_expert_ffn(t, expert, weights, w_gate, w_up, w_down):
    T, D = t.shape
    n_assign = T * TOP_K
    n_blocks = (n_assign + N_EXPERTS * (MOE_BLOCK - 1) + MOE_BLOCK - 1) // MOE_BLOCK
    n_slots = n_blocks * MOE_BLOCK
    flat_e = expert.reshape(-1)
    flat_w = weights.reshape(-1)
    flat_tok = jnp.arange(n_assign, dtype=jnp.int32) // TOP_K
    order = jnp.argsort(flat_e)
    se = flat_e[order]
    counts = jnp.zeros((N_EXPERTS,), jnp.int32).at[flat_e].add(1)
    padded = (counts + MOE_BLOCK - 1) // MOE_BLOCK * MOE_BLOCK
    starts = jnp.cumsum(counts) - counts
    pends = jnp.cumsum(padded)
    pstarts = pends - padded
    dest = pstarts[se] + (jnp.arange(n_assign, dtype=jnp.int32) - starts[se])
    slot_tok = jnp.zeros((n_slots,), jnp.int32).at[dest].set(flat_tok[order])
    slot_w = jnp.zeros((n_slots,), t.dtype).at[dest].set(flat_w[order].astype(t.dtype))
    block_start = jnp.arange(n_blocks, dtype=jnp.int32) * MOE_BLOCK
    block_e = jnp.minimum(jnp.searchsorted(pends, block_start, side='right'), N_EXPERTS - 1)
    xs = t[slot_tok].reshape(n_blocks, MOE_BLOCK, D)

    def run(args):
        xb, e = args
        hdn = jax.nn.silu(xb @ w_gate[e]) * (xb @ w_up[e])
        return hdn @ w_down[e]

    yb = lax.map(run, (xs, block_e)).reshape(n_slots, D) * slot_w[:, None]
    return jnp.zeros_like(t).at[slot_tok].add(yb)


def hierarchical_moe(h, w_grp, b_grp, w_rt, b_rt, w_gate, w_up, w_down):
    B, S, D = h.shape
    t = h.reshape(B * S, D)
    g_prob = jax.nn.softmax((t @ w_grp).astype(jnp.float32) + b_grp.astype(jnp.float32), axis=-1)
    g_p, g_idx = lax.top_k(g_prob, 1)
    e_logits = ((t @ w_rt).astype(jnp.float32) + b_rt.astype(jnp.float32)).reshape(-1, N_GROUPS, EXPERTS_PER_GROUP)
    e_sel = jnp.take_along_axis(e_logits, g_idx[:, :, None], axis=1)[:, 0]
    e_p, e_local = lax.top_k(jax.nn.softmax(e_sel, axis=-1), TOP_K)
    e_p = e_p / jnp.sum(e_p, axis=-1, keepdims=True)
    weights = g_p * e_p
    expert = g_idx * EXPERTS_PER_GROUP + e_local
    y = grouped_expert_ffn(t, expert, weights, w_gate, w_up, w_down)
    return y.reshape(B, S, D)


def setup_inputs(seed: int = 0) -> dict:
    key = jax.random.key(seed)
    ks = jax.random.split(key, 32)
    f32 = jnp.float32

    def nrm(k, shape, scale):
        return jax.random.normal(k, shape, f32) * scale

    def gain(k, shape):
        return 1.0 + 0.05 * jax.random.normal(k, shape, f32)

    a = RET_DECAY_BASE + jnp.arange(RET_HEADS, dtype=f32)
    decay_init = jnp.log(2.0 ** a - 1.0)
    return {
        "x": nrm(ks[0], (BATCH, SEQ, D_MODEL), 1.0),
        "p": nrm(ks[1], (DEPTH, BATCH, SEQ, PLE_DIM), 1.0),
        "ln_mix": gain(ks[2], (DEPTH, D_MODEL)),
        "w_in": nrm(ks[3], (DEPTH, D_MODEL, IN_WIDTH), D_MODEL ** -0.5),
        "b_gate": nrm(ks[4], (DEPTH, N_BRANCH * D_MODEL), 0.02),
        "ret_decay_logit": decay_init[None, None, :] + 0.1 * jax.random.normal(ks[5], (DEPTH, 2, RET_HEADS), f32),
        "ret_norm": gain(ks[6], (DEPTH, RET_HEADS, RET_HEAD_DIM)),
        "mla_q_norm": gain(ks[7], (DEPTH, MLA_Q_LORA)),
        "mla_w_uq": nrm(ks[8], (DEPTH, MLA_Q_LORA, MLA_HEADS * (MLA_NOPE + MLA_ROPE)), MLA_Q_LORA ** -0.5),
        "mla_kv_norm": gain(ks[9], (DEPTH, MLA_KV_LORA)),
        "mla_w_ukv": nrm(ks[10], (DEPTH, MLA_KV_LORA, MLA_HEADS * (MLA_NOPE + MLA_V)), MLA_KV_LORA ** -0.5),
        "pool_w": nrm(ks[11], (DEPTH, len(POOL_WINDOWS), POOL_GROUP, POOL_GROUP), POOL_GROUP ** -0.5),
        "pool_scale": gain(ks[12], (DEPTH, POOL_WIDTH)),
        "w_branch": nrm(ks[13], (DEPTH, N_BRANCH, BRANCH_WIDTH, D_MODEL), BRANCH_WIDTH ** -0.5),
        "w_out": nrm(ks[14], (DEPTH, D_MODEL, D_MODEL), D_MODEL ** -0.5),
        "ln_moe": gain(ks[15], (DEPTH, D_MODEL)),
        "w_grp": nrm(ks[16], (DEPTH, D_MODEL, N_GROUPS), D_MODEL ** -0.5),
        "b_grp": nrm(ks[17], (DEPTH, N_GROUPS), 0.01),
        "w_rt": nrm(ks[18], (DEPTH, D_MODEL, N_EXPERTS), D_MODEL ** -0.5),
        "b_rt": nrm(ks[19], (DEPTH, N_EXPERTS), 0.01),
        "w_exp_gate": nrm(ks[20], (DEPTH, N_EXPERTS, D_MODEL, EXPERT_HIDDEN), D_MODEL ** -0.5),
        "w_exp_up": nrm(ks[21], (DEPTH, N_EXPERTS, D_MODEL, EXPERT_HIDDEN), D_MODEL ** -0.5),
        "w_exp_down": nrm(ks[22], (DEPTH, N_EXPERTS, EXPERT_HIDDEN, D_MODEL), EXPERT_HIDDEN ** -0.5),
        "ln_ple": gain(ks[23], (DEPTH, D_MODEL)),
        "w_ple_gate": nrm(ks[24], (DEPTH, D_MODEL, D_MODEL), D_MODEL ** -0.5),
        "w_ple_proj": nrm(ks[25], (DEPTH, PLE_DIM, D_MODEL), PLE_DIM ** -0.5),
        "ln_final": gain(ks[26], (D_MODEL,)),
    }


def reference(x, p, ln_mix, w_in, b_gate, ret_decay_logit, ret_norm, mla_q_norm, mla_w_uq, mla_kv_norm,
              mla_w_ukv, pool_w, pool_scale, w_branch, w_out, ln_moe, w_grp, b_grp, w_rt, b_rt,
              w_exp_gate, w_exp_up, w_exp_down, ln_ple, w_ple_gate, w_ple_proj, ln_final):
    B, S, D = x.shape
    cos_r, sin_r = rope_tables(S, RET_HEAD_DIM)
    cos_m, sin_m = rope_tables(S, MLA_ROPE)
    for i in range(DEPTH):
        h = rmsnorm(x, ln_mix[i])
        (rq, rk, rv, rg, cq, ckv, kr, pu, gcols) = split_columns(h @ w_in[i])
        y_ret = retention_branch(rq, rk, rv, rg, ret_decay_logit[i], ret_norm[i], cos_r, sin_r)
        y_mla = mla_branch(cq, ckv, kr, mla_q_norm[i], mla_w_uq[i], mla_kv_norm[i], mla_w_ukv[i], cos_m, sin_m)
        y_pool = pool_branch(pu, pool_w[i], pool_scale[i])
        gates = jax.nn.sigmoid(gcols + b_gate[i]).reshape(B, S, N_BRANCH, D)
        merged = (gates[:, :, 0] * (y_ret @ w_branch[i, 0])
                  + gates[:, :, 1] * (y_mla @ w_branch[i, 1])
                  + gates[:, :, 2] * (y_pool @ w_branch[i, 2]))
        x = x + merged @ w_out[i]
        h = rmsnorm(x, ln_moe[i])
        x = x + hierarchical_moe(h, w_grp[i], b_grp[i], w_rt[i], b_rt[i], w_exp_gate[i], w_exp_up[i], w_exp_down[i])
        h = rmsnorm(x, ln_ple[i])
        x = x + jax.nn.sigmoid(h @ w_ple_gate[i]) * (p[i] @ w_ple_proj[i])
    return rmsnorm(x, ln_final)
```

```python
import functools
import math

import jax
import jax.numpy as jnp
from jax import lax
from jax.experimental import pallas as pl
from jax.experimental.pallas import tpu as pltpu

F32 = jnp.float32
BF16 = jnp.bfloat16

D_MODEL = 2048
RET_HEADS = 8
RET_HEAD_DIM = 128
RET_WIDTH = RET_HEADS * RET_HEAD_DIM
RET_CHUNK = 128
MLA_HEADS = 8
MLA_Q_LORA = 512
MLA_KV_LORA = 256
MLA_NOPE = 128
MLA_ROPE = 64
MLA_V = 128
MLA_QK_PAD = 256
POOL_WINDOWS = (2, 4, 8, 16)
POOL_GROUP = 256
POOL_WIDTH = len(POOL_WINDOWS) * POOL_GROUP
N_BRANCH = 3
N_GROUPS = 8
EXPERTS_PER_GROUP = 4
N_EXPERTS = N_GROUPS * EXPERTS_PER_GROUP
EXPERT_HIDDEN = 512
PAIRS = ((0, 1), (0, 2), (0, 3), (1, 2), (1, 3), (2, 3))
N_CLASSES = N_GROUPS * len(PAIRS)
PLE_DIM = 256
ROPE_BASE = 10000.0
NORM_EPS = 1e-6

COL_RET = 0
COL_CQ = 4096
COL_CKV = COL_CQ + MLA_Q_LORA
COL_KR = COL_CKV + MLA_KV_LORA
COL_POOL = 5120
COL_GATE = 6144
IN_PAD_WIDTH = COL_GATE + N_BRANCH * D_MODEL

V7X_VMEM_LIMIT = 56 * 1024 * 1024

TM = 512
TN = 1024
TM_UP = 256
TQ = 512
TK = 512
POOL_ROWS = 512
POOL_HALO = 128
MOE_BLOCK = 256
TM_ROUTER = 512


def _cparams(sem):
    return pltpu.CompilerParams(dimension_semantics=sem, vmem_limit_bytes=V7X_VMEM_LIMIT)


def _rms(x, g):
    return x * lax.rsqrt(jnp.mean(x * x, axis=-1, keepdims=True) + NORM_EPS) * g


def _dot(a, b):
    return jnp.dot(a, b, preferred_element_type=F32)


def _dot_nt(a, b):
    return lax.dot_general(a, b, (((1,), (1,)), ((), ())), preferred_element_type=F32)


def _inproj_kernel(x_ref, g_ref, w_ref, b_ref, o_ref, h_scr, *, gate_block0):
    j = pl.program_id(1)

    @pl.when(j == 0)
    def _():
        h_scr[...] = _rms(x_ref[...], g_ref[...]).astype(BF16)

    acc = _dot(h_scr[...], w_ref[...])

    @pl.when(j < gate_block0)
    def _():
        o_ref[...] = acc.astype(o_ref.dtype)

    @pl.when(j >= gate_block0)
    def _():
        o_ref[...] = jax.nn.sigmoid(acc + b_ref[...]).astype(o_ref.dtype)


def _inproj(x, ln, w, b, layer):
    T, D = x.shape
    N = w.shape[-1]
    return pl.pallas_call(
        functools.partial(_inproj_kernel, gate_block0=COL_GATE // TN),
        grid=(T // TM, N // TN),
        in_specs=[
            pl.BlockSpec((TM, D), lambda i, j: (i, 0)),
            pl.BlockSpec((None, 1, D), lambda i, j: (layer, 0, 0)),
            pl.BlockSpec((None, D, TN), lambda i, j: (layer, 0, j)),
            pl.BlockSpec((None, 1, TN), lambda i, j: (layer, 0, j)),
        ],
        out_specs=pl.BlockSpec((TM, TN), lambda i, j: (i, j)),
        out_shape=jax.ShapeDtypeStruct((T, N), BF16),
        scratch_shapes=[pltpu.VMEM((TM, D), BF16)],
        compiler_params=_cparams(("parallel", "arbitrary")),
        name="inproj",
    )(x, ln, w, b)


def _ret_kernel(q_ref, k_ref, v_ref, g_ref, cos_ref, sin_ref, lg_ref, gain_ref, o_ref,
                acc_scr, qr_scr, kr_scr, *, seq):
    C = RET_CHUNK
    n_chunks = seq // C
    lg = jax.nn.log_sigmoid(lg_ref[...])
    lgf, lgb = lg[0:1, :], lg[1:2, :]
    ii = lax.broadcasted_iota(jnp.int32, (C, C), 0).astype(F32)
    ll = lax.broadcasted_iota(jnp.int32, (C, C), 1).astype(F32)
    diff = ii - ll
    decay = jnp.where(diff >= 0, jnp.exp(lgf * jnp.maximum(diff, 0.0)),
                      jnp.exp(lgb * jnp.maximum(-diff, 0.0)))
    xi_f = jnp.exp(lgf * (ii + 1.0))
    xi_b = jnp.exp(lgb * (C - ii))
    zeta_f = jnp.exp(lgf * (C - 1.0 - ii))
    zeta_b = jnp.exp(lgb * ii)
    cdec_f = jnp.exp(lgf * C)
    cdec_b = jnp.exp(lgb * C)
    k_scale = RET_HEAD_DIM ** -0.5

    def rope(x, rows):
        return x * cos_ref[rows, :] + pltpu.roll(x, RET_HEAD_DIM // 2, 1) * sin_ref[rows, :]

    def fwd(c, state):
        rows = pl.ds(pl.multiple_of(c * C, C), C)
        q = rope(q_ref[rows, :].astype(F32), rows)
        k = rope(k_ref[rows, :].astype(F32), rows) * k_scale
        qb, kb = q.astype(BF16), k.astype(BF16)
        qr_scr[rows, :] = qb
        kr_scr[rows, :] = kb
        v = v_ref[rows, :]
        s = _dot_nt(qb, kb) * decay
        inner = _dot(s.astype(BF16), v)
        cross = _dot(qb, state.astype(BF16)) * xi_f
        acc_scr[rows, :] = inner + cross
        kv = _dot((k * zeta_f).T.astype(BF16), v)
        return state * cdec_f + kv

    lax.fori_loop(0, n_chunks, fwd, jnp.zeros((RET_HEAD_DIM, RET_HEAD_DIM), F32))

    def bwd(t, state):
        c = n_chunks - 1 - t
        rows = pl.ds(pl.multiple_of(c * C, C), C)
        qb = qr_scr[rows, :]
        v = v_ref[rows, :]
        y = acc_scr[rows, :] + _dot(qb, state.astype(BF16)) * xi_b
        yn = _rms(y, gain_ref[...])
        g = g_ref[rows, :].astype(F32)
        o_ref[rows, :] = (yn * (g * jax.nn.sigmoid(g))).astype(o_ref.dtype)
        kv = _dot((kr_scr[rows, :].astype(F32) * zeta_b).T.astype(BF16), v)
        return state * cdec_b + kv

    lax.fori_loop(0, n_chunks, bwd, jnp.zeros((RET_HEAD_DIM, RET_HEAD_DIM), F32))


def _retention(proj, cos, sin, decay_logit, gain, layer):
    B, S, _ = proj.shape
    hd = RET_HEAD_DIM
    nh = RET_HEADS

    def col(part):
        return pl.BlockSpec((None, S, hd), lambda b, h: (b, 0, COL_RET // hd + part * nh + h))

    return pl.pallas_call(
        functools.partial(_ret_kernel, seq=S),
        grid=(B, nh),
        in_specs=[
            col(0), col(1), col(2), col(3),
            pl.BlockSpec((S, hd), lambda b, h: (0, 0)),
            pl.BlockSpec((S, hd), lambda b, h: (0, 0)),
            pl.BlockSpec((None, None, 2, hd), lambda b, h: (layer, h, 0, 0)),
            pl.BlockSpec((None, None, 1, hd), lambda b, h: (layer, h, 0, 0)),
        ],
        out_specs=pl.BlockSpec((None, S, hd), lambda b, h: (b, 0, h)),
        out_shape=jax.ShapeDtypeStruct((B, S, RET_WIDTH), BF16),
        scratch_shapes=[pltpu.VMEM((S, hd), F32), pltpu.VMEM((S, hd), BF16), pltpu.VMEM((S, hd), BF16)],
        compiler_params=_cparams(("parallel", "arbitrary")),
        name="retention",
    )(proj, proj, proj, proj, cos, sin, decay_logit, gain)


def _mla_up_kernel(cq_ref, ckv_ref, kr_ref, qn_ref, kvn_ref, wq_ref, wkv_ref, cos_ref, sina_ref, sinb_ref,
                   q_out, k_out, v_out):
    scale = (MLA_NOPE + MLA_ROPE) ** -0.5
    cosp, sina, sinb = cos_ref[...], sina_ref[...], sinb_ref[...]

    def rope(x):
        return x * cosp + pltpu.roll(x, MLA_ROPE // 2, 1) * sina + pltpu.roll(x, 128 - MLA_ROPE // 2, 1) * sinb

    q = _dot(_rms(cq_ref[...].astype(F32), qn_ref[...]).astype(BF16), wq_ref[...])
    kv = _dot(_rms(ckv_ref[...].astype(F32), kvn_ref[...]).astype(BF16), wkv_ref[...])
    k_rope = rope(kr_ref[...].astype(F32)).astype(BF16)
    for h in range(MLA_HEADS):
        lo = h * MLA_QK_PAD
        q_out[:, lo:lo + MLA_NOPE] = (q[:, lo:lo + MLA_NOPE] * scale).astype(BF16)
        q_out[:, lo + MLA_NOPE:lo + MLA_QK_PAD] = (rope(q[:, lo + MLA_NOPE:lo + MLA_QK_PAD]) * scale).astype(BF16)
        k_out[:, lo:lo + MLA_NOPE] = kv[:, h * MLA_NOPE:(h + 1) * MLA_NOPE].astype(BF16)
        k_out[:, lo + MLA_NOPE:lo + MLA_QK_PAD] = k_rope
    v_out[...] = kv[:, MLA_HEADS * MLA_NOPE:].astype(BF16)


def _mla_up(proj2d, qn, kvn, wq, wkv, cosp, sina, sinb, layer, seq):
    T = proj2d.shape[0]
    tm = TM_UP
    n_s = seq // tm
    qk_w = MLA_HEADS * MLA_QK_PAD
    kv_w = MLA_HEADS * (MLA_NOPE + MLA_V)
    tab = pl.BlockSpec((tm, 128), lambda i: (i % n_s, 0))
    return pl.pallas_call(
        _mla_up_kernel,
        grid=(T // tm,),
        in_specs=[
            pl.BlockSpec((tm, MLA_Q_LORA), lambda i: (i, COL_CQ // MLA_Q_LORA)),
            pl.BlockSpec((tm, MLA_KV_LORA), lambda i: (i, COL_CKV // MLA_KV_LORA)),
            pl.BlockSpec((tm, 128), lambda i: (i, COL_KR // 128)),
            pl.BlockSpec((None, 1, MLA_Q_LORA), lambda i: (layer, 0, 0)),
            pl.BlockSpec((None, 1, MLA_KV_LORA), lambda i: (layer, 0, 0)),
            pl.BlockSpec((None, MLA_Q_LORA, qk_w), lambda i: (layer, 0, 0)),
            pl.BlockSpec((None, MLA_KV_LORA, kv_w), lambda i: (layer, 0, 0)),
            tab, tab, tab,
        ],
        out_specs=[
            pl.BlockSpec((tm, qk_w), lambda i: (i, 0)),
            pl.BlockSpec((tm, qk_w), lambda i: (i, 0)),
            pl.BlockSpec((tm, MLA_HEADS * MLA_V), lambda i: (i, 0)),
        ],
        out_shape=[
            jax.ShapeDtypeStruct((T, qk_w), BF16),
            jax.ShapeDtypeStruct((T, qk_w), BF16),
            jax.ShapeDtypeStruct((T, MLA_HEADS * MLA_V), BF16),
        ],
        compiler_params=_cparams(("parallel",)),
        name="mla_up",
    )(proj2d, proj2d, proj2d, qn, kvn, wq, wkv, cosp, sina, sinb)


def _attn_kernel(q_ref, k_ref, v_ref, o_ref, m_scr, l_scr, acc_scr, *, seq):
    q = q_ref[...]
    m_scr[...] = jnp.full(m_scr.shape, -jnp.inf, F32)
    l_scr[...] = jnp.zeros(l_scr.shape, F32)
    acc_scr[...] = jnp.zeros(acc_scr.shape, F32)

    def body(j, carry):
        rows = pl.ds(pl.multiple_of(j * TK, TK), TK)
        s = _dot_nt(q, k_ref[rows, :])
        m_old = m_scr[...]
        m_new = jnp.maximum(m_old, jnp.max(s, axis=-1, keepdims=True))
        alpha = jnp.exp(m_old - m_new)
        p = jnp.exp(s - m_new[:, 0:1])
        l_scr[...] = alpha * l_scr[...] + jnp.sum(p, axis=-1, keepdims=True)
        acc_scr[...] = alpha * acc_scr[...] + _dot(p.astype(BF16), v_ref[rows, :])
        m_scr[...] = m_new
        return carry

    lax.fori_loop(0, seq // TK, body, 0)
    o_ref[...] = (acc_scr[...] / l_scr[...]).astype(o_ref.dtype)


def _attention(q, k, v):
    B, S, _ = q.shape
    return pl.pallas_call(
        functools.partial(_attn_kernel, seq=S),
        grid=(B, MLA_HEADS, S // TQ),
        in_specs=[
            pl.BlockSpec((None, TQ, MLA_QK_PAD), lambda b, h, i: (b, i, h)),
            pl.BlockSpec((None, S, MLA_QK_PAD), lambda b, h, i: (b, 0, h)),
            pl.BlockSpec((None, S, MLA_V), lambda b, h, i: (b, 0, h)),
        ],
        out_specs=pl.BlockSpec((None, TQ, MLA_V), lambda b, h, i: (b, i, h)),
        out_shape=jax.ShapeDtypeStruct((B, S, MLA_HEADS * MLA_V), BF16),
        scratch_shapes=[pltpu.VMEM((TQ, MLA_V), F32), pltpu.VMEM((TQ, MLA_V), F32), pltpu.VMEM((TQ, MLA_V), F32)],
        compiler_params=_cparams(("parallel", "parallel", "arbitrary")),
        name="attention",
    )(q, k, v)


def _pool_kernel(u_ref, w_ref, sc_ref, o_ref, pad_scr, *, seq):
    gi = pl.program_id(1)
    half = jnp.int32(0)
    for k, w in enumerate(POOL_WINDOWS):
        half = jnp.where(gi == k, w // 2, half)
    R, H = POOL_ROWS, POOL_HALO
    pad_scr[0:H, :] = jnp.zeros((H, POOL_GROUP), BF16)
    pad_scr[H + seq:H + seq + H, :] = jnp.zeros((H, POOL_GROUP), BF16)
    pad_scr[H:H + seq, :] = u_ref[...]
    ii = lax.broadcasted_iota(jnp.int32, (R, R + 2 * H), 0)
    jj = lax.broadcasted_iota(jnp.int32, (R, R + 2 * H), 1)
    band = ((jj >= ii + H - half) & (jj < ii + H + half)).astype(BF16)
    rr = lax.broadcasted_iota(jnp.int32, (R, 1), 0)

    def body(t, carry):
        r0 = pl.multiple_of(t * R, R)
        total = _dot(band, pad_scr[pl.ds(r0, R + 2 * H), :])
        pos = r0 + rr
        cnt = (jnp.minimum(pos + half, seq) - jnp.maximum(pos - half, 0)).astype(F32)
        mixed = total / cnt - u_ref[pl.ds(r0, R), :].astype(F32)
        y = _dot(mixed.astype(BF16), w_ref[...]) * sc_ref[...]
        o_ref[pl.ds(r0, R), :] = y.astype(o_ref.dtype)
        return carry

    lax.fori_loop(0, seq // R, body, 0)


def _pool(proj, pool_w, pool_scale, layer):
    B, S, _ = proj.shape
    ng = len(POOL_WINDOWS)
    return pl.pallas_call(
        functools.partial(_pool_kernel, seq=S),
        grid=(B, ng),
        in_specs=[
            pl.BlockSpec((None, S, POOL_GROUP), lambda b, g: (b, 0, COL_POOL // POOL_GROUP + g)),
            pl.BlockSpec((None, None, POOL_GROUP, POOL_GROUP), lambda b, g: (layer, g, 0, 0)),
            pl.BlockSpec((None, 1, POOL_GROUP), lambda b, g: (layer, 0, g)),
        ],
        out_specs=pl.BlockSpec((None, S, POOL_GROUP), lambda b, g: (b, 0, g)),
        out_shape=jax.ShapeDtypeStruct((B, S, POOL_WIDTH), BF16),
        scratch_shapes=[pltpu.VMEM((S + 2 * POOL_HALO, POOL_GROUP), BF16)],
        compiler_params=_cparams(("parallel", "arbitrary")),
        name="pool",
    )(proj, pool_w, pool_scale)


def _merge_kernel(y0_ref, y1_ref, y2_ref, g0_ref, g1_ref, g2_ref, w0_ref, w1_ref, w2_ref, o_ref):
    acc = g0_ref[...].astype(F32) * _dot(y0_ref[...], w0_ref[...])
    acc += g1_ref[...].astype(F32) * _dot(y1_ref[...], w1_ref[...])
    acc += g2_ref[...].astype(F32) * _dot(y2_ref[...], w2_ref[...])
    o_ref[...] = acc.astype(o_ref.dtype)


def _merge(y_ret, y_mla, y_pool, proj2d, w_branch, layer):
    T, W = y_ret.shape
    D = D_MODEL
    ysp = pl.BlockSpec((TM, W), lambda i, j: (i, 0))

    def gate(b):
        return pl.BlockSpec((TM, TN), lambda i, j: (i, (COL_GATE + b * D) // TN + j))

    def wsp(b):
        return pl.BlockSpec((None, None, W, TN), lambda i, j: (layer, b, 0, j))

    return pl.pallas_call(
        _merge_kernel,
        grid=(T // TM, D // TN),
        in_specs=[ysp, ysp, ysp, gate(0), gate(1), gate(2), wsp(0), wsp(1), wsp(2)],
        out_specs=pl.BlockSpec((TM, TN), lambda i, j: (i, j)),
        out_shape=jax.ShapeDtypeStruct((T, D), BF16),
        compiler_params=_cparams(("parallel", "arbitrary")),
        name="merge",
    )(y_ret, y_mla, y_pool, proj2d, proj2d, proj2d, w_branch, w_branch, w_branch)


def _outproj_kernel(m_ref, w_ref, x_ref, o_ref):
    o_ref[...] = x_ref[...] + _dot(m_ref[...], w_ref[...])


def _outproj(merged, w_out, x, layer):
    T, D = x.shape
    return pl.pallas_call(
        _outproj_kernel,
        grid=(T // TM, D // TN),
        in_specs=[
            pl.BlockSpec((TM, D), lambda i, j: (i, 0)),
            pl.BlockSpec((None, D, TN), lambda i, j: (layer, 0, j)),
            pl.BlockSpec((TM, TN), lambda i, j: (i, j)),
        ],
        out_specs=pl.BlockSpec((TM, TN), lambda i, j: (i, j)),
        out_shape=jax.ShapeDtypeStruct((T, D), F32),
        compiler_params=_cparams(("parallel", "arbitrary")),
        name="outproj",
    )(merged, w_out, x)


def _router_kernel(x_ref, ln_ref, w_hi_ref, w_lo_ref, b_ref, cls_ref, wlo_ref, whi_ref):
    h = _rms(x_ref[...], ln_ref[...])
    h_hi = h.astype(BF16)
    h_lo = (h - h_hi.astype(F32)).astype(BF16)
    lt = _dot_nt(w_hi_ref[...], h_hi) + _dot_nt(w_hi_ref[...], h_lo) + _dot_nt(w_lo_ref[...], h_hi)
    lt = lt + b_ref[...]
    grp = [lt[g:g + 1, :] for g in range(N_GROUPS)]
    gmax = functools.reduce(jnp.maximum, grp)
    gden = functools.reduce(lambda a, b: a + b, [jnp.exp(x - gmax) for x in grp])
    g_p = 1.0 / gden
    gidx = jnp.full(gmax.shape, N_GROUPS, jnp.int32)
    for g in reversed(range(N_GROUPS)):
        gidx = jnp.where(grp[g] == gmax, g, gidx)
    sel = []
    for j in range(EXPERTS_PER_GROUP):
        e = jnp.zeros_like(gmax)
        for g in range(N_GROUPS):
            r = N_GROUPS + g * EXPERTS_PER_GROUP + j
            e = jnp.where(gidx == g, lt[r:r + 1, :], e)
        sel.append(e)
    emax = functools.reduce(jnp.maximum, sel)
    ex = [jnp.exp(e - emax) for e in sel]
    eden = functools.reduce(lambda a, b: a + b, ex)
    pe = [e / eden for e in ex]

    def top1(vals):
        best = functools.reduce(jnp.maximum, vals)
        idx = jnp.full(best.shape, len(vals), jnp.int32)
        for j in reversed(range(len(vals))):
            idx = jnp.where(vals[j] == best, j, idx)
        return best, idx

    p1, i1 = top1(pe)
    p2, i2 = top1([jnp.where(i1 == j, -1.0, pe[j]) for j in range(EXPERTS_PER_GROUP)])
    den = p1 + p2
    w1 = g_p * (p1 / den)
    w2 = g_p * (p2 / den)
    lo = jnp.minimum(i1, i2)
    hi = jnp.maximum(i1, i2)
    pair = jnp.zeros_like(lo)
    for n, (a, b) in enumerate(PAIRS):
        pair = jnp.where((lo == a) & (hi == b), n, pair)
    cls_ref[...] = gidx * len(PAIRS) + pair
    first_is_lo = i1 < i2
    wlo_ref[...] = jnp.where(first_is_lo, w1, w2)
    whi_ref[...] = jnp.where(first_is_lo, w2, w1)


def _router(x, ln, w_hi, w_lo, bias, layer):
    T, D = x.shape
    tm = TM_ROUTER
    n_log = N_GROUPS + N_EXPERTS
    row = pl.BlockSpec((None, 1, tm), lambda i: (i, 0, 0))
    outs = pl.pallas_call(
        _router_kernel,
        grid=(T // tm,),
        in_specs=[
            pl.BlockSpec((tm, D), lambda i: (i, 0)),
            pl.BlockSpec((None, 1, D), lambda i: (layer, 0, 0)),
            pl.BlockSpec((None, n_log, D), lambda i: (layer, 0, 0)),
            pl.BlockSpec((None, n_log, D), lambda i: (layer, 0, 0)),
            pl.BlockSpec((None, n_log, 1), lambda i: (layer, 0, 0)),
        ],
        out_specs=[row, row, row],
        out_shape=[
            jax.ShapeDtypeStruct((T // tm, 1, tm), jnp.int32),
            jax.ShapeDtypeStruct((T // tm, 1, tm), F32),
            jax.ShapeDtypeStruct((T // tm, 1, tm), F32),
        ],
        compiler_params=_cparams(("parallel",)),
        name="router",
    )(x, ln, w_hi, w_lo, bias)
    return [o.reshape(T) for o in outs]


def _moe_kernel(elo_ref, ehi_ref, nval_ref, tok_ref,
                x_hbm, wl_ref, wh_ref, ln_ref, wg_lo, wu_lo, wd_lo, wg_hi, wu_hi, wd_hi,
                o_hbm, trash_hbm, xbuf, obuf, gsem, ssem):
    del elo_ref, ehi_ref
    bm = MOE_BLOCK
    i = pl.program_id(0)
    nv = nval_ref[i]

    @pl.when(i == 0)
    def _():
        obuf[...] = jnp.zeros(obuf.shape, F32)
        fill = pltpu.make_async_copy(obuf, trash_hbm, ssem)
        fill.start()
        fill.wait()

    @pl.when(nv > 0)
    def _():
        base = i * bm

        def gather(r, carry):
            pltpu.make_async_copy(x_hbm.at[tok_ref[base + r]], xbuf.at[r], gsem).start()
            return carry

        lax.fori_loop(0, bm, gather, 0)
        pltpu.make_async_copy(x_hbm.at[pl.ds(0, bm)], xbuf, gsem).wait()
        x = xbuf[...]
        h = _rms(x, ln_ref[...]).astype(BF16)

        def ffn(wg, wu, wd):
            a = _dot(h, wg[...])
            hidden = (a * jax.nn.sigmoid(a)) * _dot(h, wu[...])
            return _dot(hidden.astype(BF16), wd[...])

        y = wl_ref[...] * ffn(wg_lo, wu_lo, wd_lo) + wh_ref[...] * ffn(wg_hi, wu_hi, wd_hi)
        obuf[...] = x + y

        def scatter(r, carry):
            @pl.when(r < nv)
            def _():
                pltpu.make_async_copy(obuf.at[r], o_hbm.at[tok_ref[base + r]], ssem).start()

            @pl.when(r >= nv)
            def _():
                pltpu.make_async_copy(obuf.at[r], trash_hbm.at[r], ssem).start()

            return carry

        lax.fori_loop(0, bm, scatter, 0)
        pltpu.make_async_copy(obuf, trash_hbm, ssem).wait()


def _moe(x, elo, ehi, nval, slot_tok, ws_lo, ws_hi, ln, w_gate, w_up, w_down, layer):
    T, D = x.shape
    bm = MOE_BLOCK
    n_blocks = nval.shape[0]
    Hd = EXPERT_HIDDEN

    def wspec(shape, which):
        def imap(i, elo_r, ehi_r, nval_r, tok_r):
            e = elo_r[i] if which == 0 else ehi_r[i]
            return (layer, e, 0, 0)
        return pl.BlockSpec((None, None) + shape, imap)

    col = pl.BlockSpec((bm, 1), lambda i, *_: (i, 0))
    grid_spec = pltpu.PrefetchScalarGridSpec(
        num_scalar_prefetch=4,
        grid=(n_blocks,),
        in_specs=[
            pl.BlockSpec(memory_space=pl.ANY),
            col, col,
            pl.BlockSpec((None, 1, D), lambda i, *_: (layer, 0, 0)),
            wspec((D, Hd), 0), wspec((D, Hd), 0), wspec((Hd, D), 0),
            wspec((D, Hd), 1), wspec((D, Hd), 1), wspec((Hd, D), 1),
        ],
        out_specs=[pl.BlockSpec(memory_space=pl.ANY), pl.BlockSpec(memory_space=pl.ANY)],
        scratch_shapes=[
            pltpu.VMEM((bm, D), F32), pltpu.VMEM((bm, D), F32),
            pltpu.SemaphoreType.DMA, pltpu.SemaphoreType.DMA,
        ],
    )
    out, _ = pl.pallas_call(
        _moe_kernel,
        grid_spec=grid_spec,
        out_shape=[jax.ShapeDtypeStruct((T, D), F32), jax.ShapeDtypeStruct((bm, D), F32)],
        compiler_params=_cparams(("arbitrary",)),
        name="moe",
    )(elo, ehi, nval, slot_tok, x, ws_lo, ws_hi, ln, w_gate, w_up, w_down, w_gate, w_up, w_down)
    return out


def _dispatch_plan(cls, wlo, whi):
    T = cls.shape[0]
    bm = MOE_BLOCK
    n_blocks = (T + N_CLASSES * (bm - 1)) // bm
    n_slots = n_blocks * bm
    order = jnp.argsort(cls).astype(jnp.int32)
    scls = cls[order]
    counts = jnp.zeros((N_CLASSES,), jnp.int32).at[cls].add(1)
    padded = (counts + bm - 1) // bm * bm
    pends = jnp.cumsum(padded)
    pstarts = pends - padded
    starts = jnp.cumsum(counts) - counts
    dest = pstarts[scls] + (jnp.arange(T, dtype=jnp.int32) - starts[scls])
    slot_tok = jnp.zeros((n_slots,), jnp.int32).at[dest].set(order)
    block_start = jnp.arange(n_blocks, dtype=jnp.int32) * bm
    n_used = pends[-1] // bm
    bidx = jnp.minimum(jnp.arange(n_blocks, dtype=jnp.int32), n_used - 1)
    bcls = jnp.minimum(jnp.searchsorted(pends, block_start[bidx], side="right"), N_CLASSES - 1).astype(jnp.int32)
    nval = jnp.clip(counts[bcls] - (block_start - pstarts[bcls]), 0, bm).astype(jnp.int32)
    nval = jnp.where(jnp.arange(n_blocks) < n_used, nval, 0)
    pair_lo = jnp.array([a for a, _ in PAIRS], jnp.int32)
    pair_hi = jnp.array([b for _, b in PAIRS], jnp.int32)
    grp = bcls // len(PAIRS)
    elo = grp * EXPERTS_PER_GROUP + pair_lo[bcls % len(PAIRS)]
    ehi = grp * EXPERTS_PER_GROUP + pair_hi[bcls % len(PAIRS)]
    ws_lo = wlo[slot_tok].reshape(n_slots, 1)
    ws_hi = whi[slot_tok].reshape(n_slots, 1)
    return elo, ehi, nval, slot_tok, ws_lo, ws_hi


def _ple_kernel(x_ref, xs_ref, p_ref, ln_ref, wg_ref, wp_ref, o_ref, h_scr):
    j = pl.program_id(1)

    @pl.when(j == 0)
    def _():
        h_scr[...] = _rms(x_ref[...], ln_ref[...]).astype(BF16)

    gate = jax.nn.sigmoid(_dot(h_scr[...], wg_ref[...]))
    o_ref[...] = xs_ref[...] + gate * _dot(p_ref[...].astype(BF16), wp_ref[...])


def _ple(x, p, ln, w_gate, w_proj, layer):
    T, D = x.shape
    return pl.pallas_call(
        _ple_kernel,
        grid=(T // TM, D // TN),
        in_specs=[
            pl.BlockSpec((TM, D), lambda i, j: (i, 0)),
            pl.BlockSpec((TM, TN), lambda i, j: (i, j)),
            pl.BlockSpec((None, TM, PLE_DIM), lambda i, j: (layer, i, 0)),
            pl.BlockSpec((None, 1, D), lambda i, j: (layer, 0, 0)),
            pl.BlockSpec((None, D, TN), lambda i, j: (layer, 0, j)),
            pl.BlockSpec((None, PLE_DIM, TN), lambda i, j: (layer, 0, j)),
        ],
        out_specs=pl.BlockSpec((TM, TN), lambda i, j: (i, j)),
        out_shape=jax.ShapeDtypeStruct((T, D), F32),
        scratch_shapes=[pltpu.VMEM((TM, D), BF16)],
        compiler_params=_cparams(("parallel", "arbitrary")),
        name="ple",
    )(x, x, p, ln, w_gate, w_proj)


def _final_norm_kernel(x_ref, g_ref, o_ref):
    o_ref[...] = _rms(x_ref[...], g_ref[...])


def _final_norm(x, g):
    T, D = x.shape
    return pl.pallas_call(
        _final_norm_kernel,
        grid=(T // TM,),
        in_specs=[pl.BlockSpec((TM, D), lambda i: (i, 0)), pl.BlockSpec((1, D), lambda i: (0, 0))],
        out_specs=pl.BlockSpec((TM, D), lambda i: (i, 0)),
        out_shape=jax.ShapeDtypeStruct((T, D), F32),
        compiler_params=_cparams(("parallel",)),
        name="final_norm",
    )(x, g)


def _rope_tables(seq, dim):
    inv = 1.0 / (ROPE_BASE ** (jnp.arange(0, dim, 2, dtype=F32) / dim))
    ang = jnp.arange(seq, dtype=F32)[:, None] * inv[None, :]
    return jnp.cos(ang), jnp.sin(ang)


def _prep_in_weights(w_in, b_gate):
    depth, D, _ = w_in.shape
    ret_w = 4 * RET_WIDTH
    mla_w = MLA_Q_LORA + MLA_KV_LORA + MLA_ROPE
    pad = COL_POOL - COL_KR - MLA_ROPE
    w = jnp.concatenate([
        w_in[..., :ret_w + mla_w],
        jnp.zeros((depth, D, pad), w_in.dtype),
        w_in[..., ret_w + mla_w:],
    ], axis=-1).astype(BF16)
    b = jnp.concatenate([jnp.zeros((depth, COL_GATE), F32), b_gate], axis=-1).reshape(depth, 1, IN_PAD_WIDTH)
    return w, b


def _prep_mla_weights(w_uq, w_ukv):
    depth = w_uq.shape[0]
    wq = w_uq.reshape(depth, MLA_Q_LORA, MLA_HEADS, MLA_NOPE + MLA_ROPE)
    wq = jnp.pad(wq, ((0, 0), (0, 0), (0, 0), (0, MLA_QK_PAD - MLA_NOPE - MLA_ROPE)))
    wq = wq.reshape(depth, MLA_Q_LORA, MLA_HEADS * MLA_QK_PAD).astype(BF16)
    wkv = w_ukv.reshape(depth, MLA_KV_LORA, MLA_HEADS, MLA_NOPE + MLA_V)
    wkv = jnp.concatenate([
        wkv[..., :MLA_NOPE].reshape(depth, MLA_KV_LORA, MLA_HEADS * MLA_NOPE),
        wkv[..., MLA_NOPE:].reshape(depth, MLA_KV_LORA, MLA_HEADS * MLA_V),
    ], axis=-1).astype(BF16)
    return wq, wkv


def _split_bf16(w):
    hi = w.astype(BF16)
    lo = (w - hi.astype(F32)).astype(BF16)
    return hi, lo


def kernel(x, p, ln_mix, w_in, b_gate, ret_decay_logit, ret_norm, mla_q_norm, mla_w_uq, mla_kv_norm, mla_w_ukv, pool_w, pool_scale, w_branch, w_out, ln_moe, w_grp, b_grp, w_rt, b_rt, w_exp_gate, w_exp_up, w_exp_down, ln_ple, w_ple_gate, w_ple_proj, ln_final):
    B, S, D = x.shape
    depth = p.shape[0]
    T = B * S
    assert D == D_MODEL and S % max(TM, TQ, TK, POOL_ROWS, TM_UP) == 0 and T % MOE_BLOCK == 0

    cos_r, sin_r = _rope_tables(S, RET_HEAD_DIM)
    ret_cos = jnp.concatenate([cos_r, cos_r], axis=-1)
    ret_sin = jnp.concatenate([-sin_r, sin_r], axis=-1)
    cos_m, sin_m = _rope_tables(S, MLA_ROPE)
    z32 = jnp.zeros_like(sin_m)
    z64 = jnp.zeros((S, 128 - MLA_ROPE), F32)
    mla_cos = jnp.concatenate([cos_m, cos_m, z64], axis=-1)
    mla_sina = jnp.concatenate([z32, sin_m, z64], axis=-1)
    mla_sinb = jnp.concatenate([-sin_m, z32, z64], axis=-1)

    w_in_p, b_in = _prep_in_weights(w_in, b_gate)
    wq, wkv = _prep_mla_weights(mla_w_uq, mla_w_ukv)
    ln_mix3 = ln_mix.reshape(depth, 1, D)
    ln_moe3 = ln_moe.reshape(depth, 1, D)
    ln_ple3 = ln_ple.reshape(depth, 1, D)
    qn3 = mla_q_norm.reshape(depth, 1, MLA_Q_LORA)
    kvn3 = mla_kv_norm.reshape(depth, 1, MLA_KV_LORA)
    decay4 = jnp.broadcast_to(ret_decay_logit.transpose(0, 2, 1)[..., None], (depth, RET_HEADS, 2, RET_HEAD_DIM))
    gain4 = ret_norm.reshape(depth, RET_HEADS, 1, RET_HEAD_DIM)
    pool_w_b = pool_w.astype(BF16)
    pool_sc3 = pool_scale.reshape(depth, 1, POOL_WIDTH)
    w_branch_b = w_branch.astype(BF16)
    w_out_b = w_out.astype(BF16)
    w_router = jnp.concatenate([w_grp, w_rt], axis=-1).transpose(0, 2, 1)
    w_router_hi, w_router_lo = _split_bf16(w_router)
    b_router = jnp.concatenate([b_grp, b_rt], axis=-1).reshape(depth, N_GROUPS + N_EXPERTS, 1)
    w_eg = w_exp_gate.astype(BF16)
    w_eu = w_exp_up.astype(BF16)
    w_ed = w_exp_down.astype(BF16)
    w_pg = w_ple_gate.astype(BF16)
    w_pp = w_ple_proj.astype(BF16)
    p3 = p.reshape(depth, T, PLE_DIM)

    xt = x.reshape(T, D)
    for layer in range(depth):
        proj = _inproj(xt, ln_mix3, w_in_p, b_in, layer)
        proj3 = proj.reshape(B, S, IN_PAD_WIDTH)
        y_ret = _retention(proj3, ret_cos, ret_sin, decay4, gain4, layer)
        q, k, v = _mla_up(proj, qn3, kvn3, wq, wkv, mla_cos, mla_sina, mla_sinb, layer, S)
        y_mla = _attention(q.reshape(B, S, -1), k.reshape(B, S, -1), v.reshape(B, S, -1))
        y_pool = _pool(proj3, pool_w_b, pool_sc3, layer)
        merged = _merge(y_ret.reshape(T, -1), y_mla.reshape(T, -1), y_pool.reshape(T, -1), proj, w_branch_b, layer)
        xt = _outproj(merged, w_out_b, xt, layer)
        cls, wlo, whi = _router(xt, ln_moe3, w_router_hi, w_router_lo, b_router, layer)
        elo, ehi, nval, slot_tok, ws_lo, ws_hi = _dispatch_plan(cls, wlo, whi)
        xt = _moe(xt, elo, ehi, nval, slot_tok, ws_lo, ws_hi, ln_moe3, w_eg, w_eu, w_ed, layer)
        xt = _ple(xt, p3, ln_ple3, w_pg, w_pp, layer)
    return _final_norm(xt, ln_final.reshape(1, D)).reshape(B, S, D)
```

```python
import functools
import math

import jax
import jax.numpy as jnp
from jax import lax
from jax.experimental import pallas as pl
from jax.experimental.pallas import tpu as pltpu

F32 = jnp.float32
BF16 = jnp.bfloat16

D_MODEL = 2048
RET_HEADS = 8
RET_HEAD_DIM = 128
RET_WIDTH = RET_HEADS * RET_HEAD_DIM
RET_CHUNK = 128
MLA_HEADS = 8
MLA_Q_LORA = 512
MLA_KV_LORA = 256
MLA_NOPE = 128
MLA_ROPE = 64
MLA_V = 128
MLA_QK_PAD = 256
POOL_WINDOWS = (2, 4, 8, 16)
POOL_GROUP = 256
POOL_WIDTH = len(POOL_WINDOWS) * POOL_GROUP
N_BRANCH = 3
N_GROUPS = 8
EXPERTS_PER_GROUP = 4
N_EXPERTS = N_GROUPS * EXPERTS_PER_GROUP
EXPERT_HIDDEN = 512
PAIRS = ((0, 1), (0, 2), (0, 3), (1, 2), (1, 3), (2, 3))
N_CLASSES = N_GROUPS * len(PAIRS)
PLE_DIM = 256
ROPE_BASE = 10000.0
NORM_EPS = 1e-6

COL_RET = 0
COL_CQ = 4096
COL_CKV = COL_CQ + MLA_Q_LORA
COL_KR = COL_CKV + MLA_KV_LORA
COL_POOL = 5120
COL_GATE = 6144
IN_PAD_WIDTH = COL_GATE + N_BRANCH * D_MODEL

V7X_VMEM_LIMIT = 56 * 1024 * 1024

TM = 512
TN = 1024
TM_UP = 256
TQ = 512
TK = 512
ATTN_CHUNKS_PER_BODY = 4
MLA_VT_ROWS = MLA_V + 16
RET_UNROLL = 8
POOL_ROWS = 512
POOL_HALO = 128
MOE_BLOCK = 256
TM_ROUTER = 512
TM_PLE = 256


def _cparams(sem):
    return pltpu.CompilerParams(dimension_semantics=sem, vmem_limit_bytes=V7X_VMEM_LIMIT)


def _rms(x, g):
    return x * lax.rsqrt(jnp.mean(x * x, axis=-1, keepdims=True) + NORM_EPS) * g


def _dot(a, b):
    return jnp.dot(a, b, preferred_element_type=F32)


def _dot_nt(a, b):
    return lax.dot_general(a, b, (((1,), (1,)), ((), ())), preferred_element_type=F32)


def _inproj_kernel(x_ref, g_ref, w_ref, b_ref, o_ref, h_scr, *, gated):
    @pl.when(pl.program_id(1) == 0)
    def _():
        h_scr[...] = _rms(x_ref[...], g_ref[...]).astype(BF16)

    acc = _dot(h_scr[...], w_ref[...])
    if gated:
        acc = jax.nn.sigmoid(acc + b_ref[...])
    o_ref[...] = acc.astype(o_ref.dtype)


def _inproj(x, ln, w, b, layer, col0, ncols, gated):
    T, D = x.shape
    j0 = col0 // TN
    return pl.pallas_call(
        functools.partial(_inproj_kernel, gated=gated),
        grid=(T // TM, ncols // TN),
        in_specs=[
            pl.BlockSpec((TM, D), lambda i, j: (i, 0)),
            pl.BlockSpec((None, 1, D), lambda i, j: (layer, 0, 0)),
            pl.BlockSpec((None, D, TN), lambda i, j: (layer, 0, j0 + j)),
            pl.BlockSpec((None, 1, TN), lambda i, j: (layer, 0, j0 + j)),
        ],
        out_specs=pl.BlockSpec((TM, TN), lambda i, j: (i, j)),
        out_shape=jax.ShapeDtypeStruct((T, ncols), BF16),
        scratch_shapes=[pltpu.VMEM((TM, D), BF16)],
        compiler_params=_cparams(("parallel", "arbitrary")),
        name="inproj_gate" if gated else "inproj",
    )(x, ln, w, b)


def _ret_kernel(q_ref, k_ref, v_ref, g_ref, cos_ref, sin_ref, lg_ref, gain_ref, o_ref,
                acc_scr, qr_scr, kr_scr, *, seq):
    C = RET_CHUNK
    n_chunks = seq // C
    lg = jax.nn.log_sigmoid(lg_ref[...])
    lgf, lgb = lg[0:1, :], lg[1:2, :]
    ii = lax.broadcasted_iota(jnp.int32, (C, C), 0).astype(F32)
    ll = lax.broadcasted_iota(jnp.int32, (C, C), 1).astype(F32)
    diff = ii - ll
    decay = jnp.where(diff >= 0, jnp.exp(lgf * jnp.maximum(diff, 0.0)),
                      jnp.exp(lgb * jnp.maximum(-diff, 0.0)))
    xi_f = jnp.exp(lgf * (ii + 1.0))
    xi_b = jnp.exp(lgb * (C - ii))
    zeta_f = jnp.exp(lgf * (C - 1.0 - ii))
    zeta_b = jnp.exp(lgb * ii)
    cdec_f = jnp.exp(lgf * C)
    cdec_b = jnp.exp(lgb * C)
    k_scale = RET_HEAD_DIM ** -0.5

    def rope(x, rows):
        return x * cos_ref[rows, :] + pltpu.roll(x, RET_HEAD_DIM // 2, 1) * sin_ref[rows, :]

    def fwd(c, state):
        rows = pl.ds(pl.multiple_of(c * C, C), C)
        q = rope(q_ref[rows, :].astype(F32), rows)
        k = rope(k_ref[rows, :].astype(F32), rows) * k_scale
        qb, kb = q.astype(BF16), k.astype(BF16)
        qr_scr[rows, :] = qb
        kr_scr[rows, :] = kb
        v = v_ref[rows, :]
        s = _dot_nt(qb, kb) * decay
        inner = _dot(s.astype(BF16), v)
        cross = _dot(qb, state.astype(BF16)) * xi_f
        acc_scr[rows, :] = inner + cross
        kv = _dot((k * zeta_f).T.astype(BF16), v)
        return state * cdec_f + kv

    lax.fori_loop(0, n_chunks, fwd, jnp.zeros((RET_HEAD_DIM, RET_HEAD_DIM), F32), unroll=RET_UNROLL)

    def bwd(t, state):
        c = n_chunks - 1 - t
        rows = pl.ds(pl.multiple_of(c * C, C), C)
        qb = qr_scr[rows, :]
        v = v_ref[rows, :]
        y = acc_scr[rows, :] + _dot(qb, state.astype(BF16)) * xi_b
        yn = _rms(y, gain_ref[...])
        g = g_ref[rows, :].astype(F32)
        o_ref[rows, :] = (yn * (g * jax.nn.sigmoid(g))).astype(o_ref.dtype)
        kv = _dot((kr_scr[rows, :].astype(F32) * zeta_b).T.astype(BF16), v)
        return state * cdec_b + kv

    lax.fori_loop(0, n_chunks, bwd, jnp.zeros((RET_HEAD_DIM, RET_HEAD_DIM), F32), unroll=RET_UNROLL)


def _retention(proj, cos, sin, decay_logit, gain, layer):
    B, S, _ = proj.shape
    hd = RET_HEAD_DIM
    nh = RET_HEADS

    def col(part):
        return pl.BlockSpec((None, S, hd), lambda b, h: (b, 0, COL_RET // hd + part * nh + h))

    return pl.pallas_call(
        functools.partial(_ret_kernel, seq=S),
        grid=(B, nh),
        in_specs=[
            col(0), col(1), col(2), col(3),
            pl.BlockSpec((S, hd), lambda b, h: (0, 0)),
            pl.BlockSpec((S, hd), lambda b, h: (0, 0)),
            pl.BlockSpec((None, None, 2, hd), lambda b, h: (layer, h, 0, 0)),
            pl.BlockSpec((None, None, 1, hd), lambda b, h: (layer, h, 0, 0)),
        ],
        out_specs=pl.BlockSpec((None, S, hd), lambda b, h: (b, 0, h)),
        out_shape=jax.ShapeDtypeStruct((B, S, RET_WIDTH), BF16),
        scratch_shapes=[pltpu.VMEM((S, hd), F32), pltpu.VMEM((S, hd), BF16), pltpu.VMEM((S, hd), BF16)],
        compiler_params=_cparams(("parallel", "arbitrary")),
        name="retention",
    )(proj, proj, proj, proj, cos, sin, decay_logit, gain)


def _mla_up_kernel(cq_ref, ckv_ref, kr_ref, qn_ref, kvn_ref, wq_ref, wkv_ref, cos_ref, sina_ref, sinb_ref,
                   q_out, k_out, vt_out):
    scale = (MLA_NOPE + MLA_ROPE) ** -0.5 * math.log2(math.e)
    cosp, sina, sinb = cos_ref[...], sina_ref[...], sinb_ref[...]

    def rope(x):
        return x * cosp + pltpu.roll(x, MLA_ROPE // 2, 1) * sina + pltpu.roll(x, 128 - MLA_ROPE // 2, 1) * sinb

    q = _dot(_rms(cq_ref[...].astype(F32), qn_ref[...]).astype(BF16), wq_ref[...])
    kv = _dot(_rms(ckv_ref[...].astype(F32), kvn_ref[...]).astype(BF16), wkv_ref[...])
    k_rope = rope(kr_ref[...].astype(F32)).astype(BF16)
    for h in range(MLA_HEADS):
        lo = h * MLA_QK_PAD
        q_out[:, lo:lo + MLA_NOPE] = (q[:, lo:lo + MLA_NOPE] * scale).astype(BF16)
        q_out[:, lo + MLA_NOPE:lo + MLA_QK_PAD] = (rope(q[:, lo + MLA_NOPE:lo + MLA_QK_PAD]) * scale).astype(BF16)
        k_out[:, lo:lo + MLA_NOPE] = kv[:, h * MLA_NOPE:(h + 1) * MLA_NOPE].astype(BF16)
        k_out[:, lo + MLA_NOPE:lo + MLA_QK_PAD] = k_rope
    v_t = kv[:, MLA_HEADS * MLA_NOPE:].T
    ones = jnp.ones((MLA_VT_ROWS - MLA_V, v_t.shape[1]), BF16)
    for h in range(MLA_HEADS):
        lo = h * MLA_VT_ROWS
        vt_out[lo:lo + MLA_V, :] = v_t[h * MLA_V:(h + 1) * MLA_V, :].astype(BF16)
        vt_out[lo + MLA_V:lo + MLA_VT_ROWS, :] = ones


def _mla_up(proj2d, qn, kvn, wq, wkv, cosp, sina, sinb, layer, seq):
    T = proj2d.shape[0]
    tm = TM_UP
    n_s = seq // tm
    qk_w = MLA_HEADS * MLA_QK_PAD
    kv_w = MLA_HEADS * (MLA_NOPE + MLA_V)
    tab = pl.BlockSpec((tm, 128), lambda i: (i % n_s, 0))
    return pl.pallas_call(
        _mla_up_kernel,
        grid=(T // tm,),
        in_specs=[
            pl.BlockSpec((tm, MLA_Q_LORA), lambda i: (i, COL_CQ // MLA_Q_LORA)),
            pl.BlockSpec((tm, MLA_KV_LORA), lambda i: (i, COL_CKV // MLA_KV_LORA)),
            pl.BlockSpec((tm, 128), lambda i: (i, COL_KR // 128)),
            pl.BlockSpec((None, 1, MLA_Q_LORA), lambda i: (layer, 0, 0)),
            pl.BlockSpec((None, 1, MLA_KV_LORA), lambda i: (layer, 0, 0)),
            pl.BlockSpec((None, MLA_Q_LORA, qk_w), lambda i: (layer, 0, 0)),
            pl.BlockSpec((None, MLA_KV_LORA, kv_w), lambda i: (layer, 0, 0)),
            tab, tab, tab,
        ],
        out_specs=[
            pl.BlockSpec((tm, qk_w), lambda i: (i, 0)),
            pl.BlockSpec((tm, qk_w), lambda i: (i, 0)),
            pl.BlockSpec((None, MLA_HEADS * MLA_VT_ROWS, tm), lambda i: (i // n_s, 0, i % n_s)),
        ],
        out_shape=[
            jax.ShapeDtypeStruct((T, qk_w), BF16),
            jax.ShapeDtypeStruct((T, qk_w), BF16),
            jax.ShapeDtypeStruct((T // seq, MLA_HEADS * MLA_VT_ROWS, seq), BF16),
        ],
        compiler_params=_cparams(("parallel",)),
        name="mla_up",
    )(proj2d, proj2d, proj2d, qn, kvn, wq, wkv, cosp, sina, sinb)


def _attn_kernel(q_ref, k_ref, vt_ref, o_ref, acc_scr, st_a, st_b, *, seq):
    n = seq // TK
    U = ATTN_CHUNKS_PER_BODY
    q = q_ref[...]
    acc_scr[...] = jnp.zeros(acc_scr.shape, F32)
    bufs = (st_a, st_b)

    def scores(c, dst):
        r0 = pl.multiple_of(c * TK, TK)
        dst[...] = _dot_nt(k_ref[pl.ds(r0, TK), :], q)

    def absorb(c, src, m):
        r0 = pl.multiple_of(c * TK, TK)
        st = src[...]
        m_new = jnp.maximum(m, jnp.max(st, axis=0, keepdims=True))
        alpha = jnp.exp2(m - m_new)
        pt = jnp.exp2((st - m_new).astype(BF16))
        acc_scr[...] = alpha * acc_scr[...] + _dot(vt_ref[:, pl.ds(r0, TK)], pt)
        return m_new

    scores(0, st_a)

    def body(jj, m):
        for u in range(U):
            scores(U * jj + u + 1, bufs[(u + 1) % 2])
            m = absorb(U * jj + u, bufs[u % 2], m)
        return m

    m = lax.fori_loop(0, n // U - 1, body, jnp.full((1, TQ), -jnp.inf, F32))
    for u in range(U):
        c = n - U + u
        if u + 1 < U:
            scores(c + 1, bufs[(u + 1) % 2])
        m = absorb(c, bufs[u % 2], m)
    acc = acc_scr[...]
    out_t = acc[0:MLA_V, :] / acc[MLA_V:MLA_V + 1, :]
    o_ref[...] = out_t.T.astype(o_ref.dtype)


def _attention(q, k, vt):
    B, S, _ = q.shape
    assert ATTN_CHUNKS_PER_BODY % 2 == 0 and (S // TK) % ATTN_CHUNKS_PER_BODY == 0
    return pl.pallas_call(
        functools.partial(_attn_kernel, seq=S),
        grid=(B, MLA_HEADS, S // TQ),
        in_specs=[
            pl.BlockSpec((None, TQ, MLA_QK_PAD), lambda b, h, i: (b, i, h)),
            pl.BlockSpec((None, S, MLA_QK_PAD), lambda b, h, i: (b, 0, h)),
            pl.BlockSpec((None, MLA_VT_ROWS, S), lambda b, h, i: (b, h, 0)),
        ],
        out_specs=pl.BlockSpec((None, TQ, MLA_V), lambda b, h, i: (b, i, h)),
        out_shape=jax.ShapeDtypeStruct((B, S, MLA_HEADS * MLA_V), BF16),
        scratch_shapes=[pltpu.VMEM((MLA_VT_ROWS, TQ), F32), pltpu.VMEM((TK, TQ), F32), pltpu.VMEM((TK, TQ), F32)],
        compiler_params=_cparams(("parallel", "parallel", "arbitrary")),
        name="attention",
    )(q, k, vt)


def _pool_kernel(u_ref, w_ref, sc_ref, o_ref, pad_scr, *, seq):
    gi = pl.program_id(1)
    half = jnp.int32(0)
    for k, w in enumerate(POOL_WINDOWS):
        half = jnp.where(gi == k, w // 2, half)
    R, H = POOL_ROWS, POOL_HALO
    pad_scr[0:H, :] = jnp.zeros((H, POOL_GROUP), BF16)
    pad_scr[H + seq:H + seq + H, :] = jnp.zeros((H, POOL_GROUP), BF16)
    pad_scr[H:H + seq, :] = u_ref[...]
    ii = lax.broadcasted_iota(jnp.int32, (R, R + 2 * H), 0)
    jj = lax.broadcasted_iota(jnp.int32, (R, R + 2 * H), 1)
    band = ((jj >= ii + H - half) & (jj < ii + H + half)).astype(BF16)
    rr = lax.broadcasted_iota(jnp.int32, (R, 1), 0)

    def body(t, carry):
        r0 = pl.multiple_of(t * R, R)
        total = _dot(band, pad_scr[pl.ds(r0, R + 2 * H), :])
        pos = r0 + rr
        cnt = (jnp.minimum(pos + half, seq) - jnp.maximum(pos - half, 0)).astype(F32)
        mixed = total / cnt - u_ref[pl.ds(r0, R), :].astype(F32)
        y = _dot(mixed.astype(BF16), w_ref[...]) * sc_ref[...]
        o_ref[pl.ds(r0, R), :] = y.astype(o_ref.dtype)
        return carry

    lax.fori_loop(0, seq // R, body, 0)


def _pool(proj, pool_w, pool_scale, layer):
    B, S, _ = proj.shape
    ng = len(POOL_WINDOWS)
    return pl.pallas_call(
        functools.partial(_pool_kernel, seq=S),
        grid=(B, ng),
        in_specs=[
            pl.BlockSpec((None, S, POOL_GROUP), lambda b, g: (b, 0, COL_POOL // POOL_GROUP + g)),
            pl.BlockSpec((None, None, POOL_GROUP, POOL_GROUP), lambda b, g: (layer, g, 0, 0)),
            pl.BlockSpec((None, 1, POOL_GROUP), lambda b, g: (layer, 0, g)),
        ],
        out_specs=pl.BlockSpec((None, S, POOL_GROUP), lambda b, g: (b, 0, g)),
        out_shape=jax.ShapeDtypeStruct((B, S, POOL_WIDTH), BF16),
        scratch_shapes=[pltpu.VMEM((S + 2 * POOL_HALO, POOL_GROUP), BF16)],
        compiler_params=_cparams(("parallel", "arbitrary")),
        name="pool",
    )(proj, pool_w, pool_scale)


def _merge_kernel(y0_ref, y1_ref, y2_ref, g0_ref, g1_ref, g2_ref, w0_ref, w1_ref, w2_ref, o_ref):
    acc = g0_ref[...].astype(F32) * _dot(y0_ref[...], w0_ref[...])
    acc += g1_ref[...].astype(F32) * _dot(y1_ref[...], w1_ref[...])
    acc += g2_ref[...].astype(F32) * _dot(y2_ref[...], w2_ref[...])
    o_ref[...] = acc.astype(o_ref.dtype)


def _merge(y_ret, y_mla, y_pool, gates, w_branch, layer):
    T, W = y_ret.shape
    D = D_MODEL
    ysp = pl.BlockSpec((TM, W), lambda i, j: (i, 0))

    def gate(b):
        return pl.BlockSpec((TM, TN), lambda i, j: (i, b * D // TN + j))

    def wsp(b):
        return pl.BlockSpec((None, None, W, TN), lambda i, j: (layer, b, 0, j))

    return pl.pallas_call(
        _merge_kernel,
        grid=(T // TM, D // TN),
        in_specs=[ysp, ysp, ysp, gate(0), gate(1), gate(2), wsp(0), wsp(1), wsp(2)],
        out_specs=pl.BlockSpec((TM, TN), lambda i, j: (i, j)),
        out_shape=jax.ShapeDtypeStruct((T, D), BF16),
        compiler_params=_cparams(("parallel", "arbitrary")),
        name="merge",
    )(y_ret, y_mla, y_pool, gates, gates, gates, w_branch, w_branch, w_branch)


def _outproj_kernel(m_ref, w_ref, x_ref, o_ref):
    o_ref[...] = x_ref[...] + _dot(m_ref[...], w_ref[...])


def _outproj(merged, w_out, x, layer):
    T, D = x.shape
    return pl.pallas_call(
        _outproj_kernel,
        grid=(T // TM, D // TN),
        in_specs=[
            pl.BlockSpec((TM, D), lambda i, j: (i, 0)),
            pl.BlockSpec((None, D, TN), lambda i, j: (layer, 0, j)),
            pl.BlockSpec((TM, TN), lambda i, j: (i, j)),
        ],
        out_specs=pl.BlockSpec((TM, TN), lambda i, j: (i, j)),
        out_shape=jax.ShapeDtypeStruct((T, D), F32),
        compiler_params=_cparams(("parallel", "arbitrary")),
        name="outproj",
    )(merged, w_out, x)


def _router_kernel(x_ref, ln_ref, w_hi_ref, w_lo_ref, b_ref, cls_ref, wlo_ref, whi_ref):
    h = _rms(x_ref[...], ln_ref[...])
    h_hi = h.astype(BF16)
    h_lo = (h - h_hi.astype(F32)).astype(BF16)
    lt = _dot_nt(w_hi_ref[...], h_hi) + _dot_nt(w_hi_ref[...], h_lo) + _dot_nt(w_lo_ref[...], h_hi)
    lt = lt + b_ref[...]
    grp = [lt[g:g + 1, :] for g in range(N_GROUPS)]
    gmax = functools.reduce(jnp.maximum, grp)
    gden = functools.reduce(lambda a, b: a + b, [jnp.exp(x - gmax) for x in grp])
    g_p = 1.0 / gden
    gidx = jnp.full(gmax.shape, N_GROUPS, jnp.int32)
    for g in reversed(range(N_GROUPS)):
        gidx = jnp.where(grp[g] == gmax, g, gidx)
    sel = []
    for j in range(EXPERTS_PER_GROUP):
        e = jnp.zeros_like(gmax)
        for g in range(N_GROUPS):
            r = N_GROUPS + g * EXPERTS_PER_GROUP + j
            e = jnp.where(gidx == g, lt[r:r + 1, :], e)
        sel.append(e)
    emax = functools.reduce(jnp.maximum, sel)
    ex = [jnp.exp(e - emax) for e in sel]
    eden = functools.reduce(lambda a, b: a + b, ex)
    pe = [e / eden for e in ex]

    def top1(vals):
        best = functools.reduce(jnp.maximum, vals)
        idx = jnp.full(best.shape, len(vals), jnp.int32)
        for j in reversed(range(len(vals))):
            idx = jnp.where(vals[j] == best, j, idx)
        return best, idx

    p1, i1 = top1(pe)
    p2, i2 = top1([jnp.where(i1 == j, -1.0, pe[j]) for j in range(EXPERTS_PER_GROUP)])
    den = p1 + p2
    w1 = g_p * (p1 / den)
    w2 = g_p * (p2 / den)
    lo = jnp.minimum(i1, i2)
    hi = jnp.maximum(i1, i2)
    pair = jnp.zeros_like(lo)
    for n, (a, b) in enumerate(PAIRS):
        pair = jnp.where((lo == a) & (hi == b), n, pair)
    cls_ref[...] = gidx * len(PAIRS) + pair
    first_is_lo = i1 < i2
    wlo_ref[...] = jnp.where(first_is_lo, w1, w2)
    whi_ref[...] = jnp.where(first_is_lo, w2, w1)


def _router(x, ln, w_hi, w_lo, bias, layer):
    T, D = x.shape
    tm = TM_ROUTER
    n_log = N_GROUPS + N_EXPERTS
    row = pl.BlockSpec((None, 1, tm), lambda i: (i, 0, 0))
    outs = pl.pallas_call(
        _router_kernel,
        grid=(T // tm,),
        in_specs=[
            pl.BlockSpec((tm, D), lambda i: (i, 0)),
            pl.BlockSpec((None, 1, D), lambda i: (layer, 0, 0)),
            pl.BlockSpec((None, n_log, D), lambda i: (layer, 0, 0)),
            pl.BlockSpec((None, n_log, D), lambda i: (layer, 0, 0)),
            pl.BlockSpec((None, n_log, 1), lambda i: (layer, 0, 0)),
        ],
        out_specs=[row, row, row],
        out_shape=[
            jax.ShapeDtypeStruct((T // tm, 1, tm), jnp.int32),
            jax.ShapeDtypeStruct((T // tm, 1, tm), F32),
            jax.ShapeDtypeStruct((T // tm, 1, tm), F32),
        ],
        compiler_params=_cparams(("parallel",)),
        name="router",
    )(x, ln, w_hi, w_lo, bias)
    return [o.reshape(T) for o in outs]


def _moe_kernel(elo_ref, ehi_ref, nused_ref, tok_ref, dst_ref,
                x_hbm, wl_ref, wh_ref, ln_ref, wg_lo, wu_lo, wd_lo, wg_hi, wu_hi, wd_hi,
                o_hbm, xbuf, obuf, gsem, ssem):
    del elo_ref, ehi_ref
    bm = MOE_BLOCK
    i = pl.program_id(0)
    n_used = nused_ref[0]
    cur = i % 2

    def start_gather(block, buf):
        for r in range(bm):
            pltpu.make_async_copy(x_hbm.at[tok_ref[block * bm + r]], xbuf.at[buf, r], gsem.at[buf]).start()

    def wait_gather(buf):
        pltpu.make_async_copy(x_hbm.at[pl.ds(0, bm)], xbuf.at[buf], gsem.at[buf]).wait()

    def start_scatter(block, buf):
        for r in range(bm):
            pltpu.make_async_copy(obuf.at[buf, r], o_hbm.at[dst_ref[(block + 1) * bm + r]], ssem).start()

    def wait_scatter():
        pltpu.make_async_copy(obuf.at[0], o_hbm.at[pl.ds(0, bm)], ssem).wait()

    @pl.when(i == 0)
    def _():
        obuf[...] = jnp.zeros(obuf.shape, F32)
        start_gather(0, 0)
        start_scatter(-1, 1)

    @pl.when(i < n_used)
    def _():
        wait_gather(cur)
        wait_scatter()
        start_gather(jnp.minimum(i + 1, n_used - 1), 1 - cur)
        start_scatter(i - 1, 1 - cur)
        x = xbuf[cur]
        h = _rms(x, ln_ref[...]).astype(BF16)

        def ffn(wg, wu, wd):
            a = _dot(h, wg[...])
            hidden = (a * jax.nn.sigmoid(a)) * _dot(h, wu[...])
            return _dot(hidden.astype(BF16), wd[...])

        y = wl_ref[...] * ffn(wg_lo, wu_lo, wd_lo) + wh_ref[...] * ffn(wg_hi, wu_hi, wd_hi)
        obuf[cur] = x + y

        @pl.when(i == n_used - 1)
        def _():
            wait_scatter()
            start_scatter(i, cur)
            wait_scatter()
            wait_gather(1 - cur)


def _moe(x, elo, ehi, n_used, slot_tok, slot_dst, ws_lo, ws_hi, ln, w_gate, w_up, w_down, layer):
    T, D = x.shape
    bm = MOE_BLOCK
    n_blocks = elo.shape[0]
    Hd = EXPERT_HIDDEN

    def wspec(shape, which):
        def imap(i, elo_r, ehi_r, *_):
            e = elo_r[i] if which == 0 else ehi_r[i]
            return (layer, e, 0, 0)
        return pl.BlockSpec((None, None) + shape, imap)

    col = pl.BlockSpec((bm, 1), lambda i, *_: (i, 0))
    grid_spec = pltpu.PrefetchScalarGridSpec(
        num_scalar_prefetch=5,
        grid=(n_blocks,),
        in_specs=[
            pl.BlockSpec(memory_space=pl.ANY),
            col, col,
            pl.BlockSpec((None, 1, D), lambda i, *_: (layer, 0, 0)),
            wspec((D, Hd), 0), wspec((D, Hd), 0), wspec((Hd, D), 0),
            wspec((D, Hd), 1), wspec((D, Hd), 1), wspec((Hd, D), 1),
        ],
        out_specs=pl.BlockSpec(memory_space=pl.ANY),
        scratch_shapes=[
            pltpu.VMEM((2, bm, D), F32), pltpu.VMEM((2, bm, D), F32),
            pltpu.SemaphoreType.DMA((2,)), pltpu.SemaphoreType.DMA,
        ],
    )
    return pl.pallas_call(
        _moe_kernel,
        grid_spec=grid_spec,
        out_shape=jax.ShapeDtypeStruct((T + bm, D), F32),
        compiler_params=_cparams(("arbitrary",)),
        name="moe",
    )(elo, ehi, n_used, slot_tok, slot_dst, x, ws_lo, ws_hi, ln, w_gate, w_up, w_down, w_gate, w_up, w_down)


def _dispatch_plan(cls, wlo, whi):
    T = cls.shape[0]
    bm = MOE_BLOCK
    n_blocks = (T + N_CLASSES * (bm - 1)) // bm
    n_slots = n_blocks * bm
    order = jnp.argsort(cls).astype(jnp.int32)
    scls = cls[order]
    counts = jnp.zeros((N_CLASSES,), jnp.int32).at[cls].add(1)
    padded = (counts + bm - 1) // bm * bm
    pends = jnp.cumsum(padded)
    pstarts = pends - padded
    starts = jnp.cumsum(counts) - counts
    dest = pstarts[scls] + (jnp.arange(T, dtype=jnp.int32) - starts[scls])
    slot_tok = jnp.zeros((n_slots,), jnp.int32).at[dest].set(order)
    spare = T + jnp.arange(n_slots, dtype=jnp.int32) % bm
    slot_dst = jnp.concatenate([spare[:bm], spare.at[dest].set(order)])
    block_start = jnp.arange(n_blocks, dtype=jnp.int32) * bm
    n_used = pends[-1] // bm
    bidx = jnp.minimum(jnp.arange(n_blocks, dtype=jnp.int32), n_used - 1)
    bcls = jnp.minimum(jnp.searchsorted(pends, block_start[bidx], side="right"), N_CLASSES - 1).astype(jnp.int32)
    pair_lo = jnp.array([a for a, _ in PAIRS], jnp.int32)
    pair_hi = jnp.array([b for _, b in PAIRS], jnp.int32)
    grp = bcls // len(PAIRS)
    elo = grp * EXPERTS_PER_GROUP + pair_lo[bcls % len(PAIRS)]
    ehi = grp * EXPERTS_PER_GROUP + pair_hi[bcls % len(PAIRS)]
    ws_lo = wlo[slot_tok].reshape(n_slots, 1)
    ws_hi = whi[slot_tok].reshape(n_slots, 1)
    return elo, ehi, n_used.reshape(1).astype(jnp.int32), slot_tok, slot_dst, ws_lo, ws_hi


def _ple_kernel(x_ref, p_ref, ln_ref, wg_ref, wp_ref, fin_ref, o_ref, *, final_norm):
    x = x_ref[...]
    gate = jax.nn.sigmoid(_dot(_rms(x, ln_ref[...]).astype(BF16), wg_ref[...]))
    y = x + gate * _dot(p_ref[...].astype(BF16), wp_ref[...])
    o_ref[...] = _rms(y, fin_ref[...]) if final_norm else y


def _ple(x, p, ln, w_gate, w_proj, fin_gain, layer, final_norm):
    T, D = p.shape[1], x.shape[1]
    return pl.pallas_call(
        functools.partial(_ple_kernel, final_norm=final_norm),
        grid=(T // TM_PLE,),
        in_specs=[
            pl.BlockSpec((TM_PLE, D), lambda i: (i, 0)),
            pl.BlockSpec((None, TM_PLE, PLE_DIM), lambda i: (layer, i, 0)),
            pl.BlockSpec((None, 1, D), lambda i: (layer, 0, 0)),
            pl.BlockSpec((None, D, D), lambda i: (layer, 0, 0)),
            pl.BlockSpec((None, PLE_DIM, D), lambda i: (layer, 0, 0)),
            pl.BlockSpec((1, D), lambda i: (0, 0)),
        ],
        out_specs=pl.BlockSpec((TM_PLE, D), lambda i: (i, 0)),
        out_shape=jax.ShapeDtypeStruct((T, D), F32),
        compiler_params=_cparams(("parallel",)),
        name="ple_final" if final_norm else "ple",
    )(x, p, ln, w_gate, w_proj, fin_gain)


def _rope_tables(seq, dim):
    inv = 1.0 / (ROPE_BASE ** (jnp.arange(0, dim, 2, dtype=F32) / dim))
    ang = jnp.arange(seq, dtype=F32)[:, None] * inv[None, :]
    return jnp.cos(ang), jnp.sin(ang)


def _prep_in_weights(w_in, b_gate):
    depth, D, _ = w_in.shape
    ret_w = 4 * RET_WIDTH
    mla_w = MLA_Q_LORA + MLA_KV_LORA + MLA_ROPE
    pad = COL_POOL - COL_KR - MLA_ROPE
    w = jnp.concatenate([
        w_in[..., :ret_w + mla_w],
        jnp.zeros((depth, D, pad), w_in.dtype),
        w_in[..., ret_w + mla_w:],
    ], axis=-1).astype(BF16)
    b = jnp.concatenate([jnp.zeros((depth, COL_GATE), F32), b_gate], axis=-1).reshape(depth, 1, IN_PAD_WIDTH)
    return w, b


def _prep_mla_weights(w_uq, w_ukv):
    depth = w_uq.shape[0]
    wq = w_uq.reshape(depth, MLA_Q_LORA, MLA_HEADS, MLA_NOPE + MLA_ROPE)
    wq = jnp.pad(wq, ((0, 0), (0, 0), (0, 0), (0, MLA_QK_PAD - MLA_NOPE - MLA_ROPE)))
    wq = wq.reshape(depth, MLA_Q_LORA, MLA_HEADS * MLA_QK_PAD).astype(BF16)
    wkv = w_ukv.reshape(depth, MLA_KV_LORA, MLA_HEADS, MLA_NOPE + MLA_V)
    wkv = jnp.concatenate([
        wkv[..., :MLA_NOPE].reshape(depth, MLA_KV_LORA, MLA_HEADS * MLA_NOPE),
        wkv[..., MLA_NOPE:].reshape(depth, MLA_KV_LORA, MLA_HEADS * MLA_V),
    ], axis=-1).astype(BF16)
    return wq, wkv


def _split_bf16(w):
    hi = w.astype(BF16)
    lo = (w - hi.astype(F32)).astype(BF16)
    return hi, lo


def kernel(x, p, ln_mix, w_in, b_gate, ret_decay_logit, ret_norm, mla_q_norm, mla_w_uq, mla_kv_norm, mla_w_ukv, pool_w, pool_scale, w_branch, w_out, ln_moe, w_grp, b_grp, w_rt, b_rt, w_exp_gate, w_exp_up, w_exp_down, ln_ple, w_ple_gate, w_ple_proj, ln_final):
    B, S, D = x.shape
    depth = p.shape[0]
    T = B * S
    assert D == D_MODEL and S % max(TM, TQ, TK, POOL_ROWS, TM_UP) == 0 and T % MOE_BLOCK == 0

    cos_r, sin_r = _rope_tables(S, RET_HEAD_DIM)
    ret_cos = jnp.concatenate([cos_r, cos_r], axis=-1)
    ret_sin = jnp.concatenate([-sin_r, sin_r], axis=-1)
    cos_m, sin_m = _rope_tables(S, MLA_ROPE)
    z32 = jnp.zeros_like(sin_m)
    z64 = jnp.zeros((S, 128 - MLA_ROPE), F32)
    mla_cos = jnp.concatenate([cos_m, cos_m, z64], axis=-1)
    mla_sina = jnp.concatenate([z32, sin_m, z64], axis=-1)
    mla_sinb = jnp.concatenate([-sin_m, z32, z64], axis=-1)

    w_in_p, b_in = _prep_in_weights(w_in, b_gate)
    wq, wkv = _prep_mla_weights(mla_w_uq, mla_w_ukv)
    ln_mix3 = ln_mix.reshape(depth, 1, D)
    ln_moe3 = ln_moe.reshape(depth, 1, D)
    ln_ple3 = ln_ple.reshape(depth, 1, D)
    qn3 = mla_q_norm.reshape(depth, 1, MLA_Q_LORA)
    kvn3 = mla_kv_norm.reshape(depth, 1, MLA_KV_LORA)
    decay4 = jnp.broadcast_to(ret_decay_logit.transpose(0, 2, 1)[..., None], (depth, RET_HEADS, 2, RET_HEAD_DIM))
    gain4 = ret_norm.reshape(depth, RET_HEADS, 1, RET_HEAD_DIM)
    pool_w_b = pool_w.astype(BF16)
    pool_sc3 = pool_scale.reshape(depth, 1, POOL_WIDTH)
    w_branch_b = w_branch.astype(BF16)
    w_out_b = w_out.astype(BF16)
    w_router = jnp.concatenate([w_grp, w_rt], axis=-1).transpose(0, 2, 1)
    w_router_hi, w_router_lo = _split_bf16(w_router)
    b_router = jnp.concatenate([b_grp, b_rt], axis=-1).reshape(depth, N_GROUPS + N_EXPERTS, 1)
    w_eg = w_exp_gate.astype(BF16)
    w_eu = w_exp_up.astype(BF16)
    w_ed = w_exp_down.astype(BF16)
    w_pg = w_ple_gate.astype(BF16)
    w_pp = w_ple_proj.astype(BF16)
    p3 = p.reshape(depth, T, PLE_DIM)

    xt = x.reshape(T, D)
    for layer in range(depth):
        proj = _inproj(xt, ln_mix3, w_in_p, b_in, layer, 0, COL_GATE, False)
        gates = _inproj(xt, ln_mix3, w_in_p, b_in, layer, COL_GATE, N_BRANCH * D, True)
        proj3 = proj.reshape(B, S, COL_GATE)
        y_ret = _retention(proj3, ret_cos, ret_sin, decay4, gain4, layer)
        q, k, vt = _mla_up(proj, qn3, kvn3, wq, wkv, mla_cos, mla_sina, mla_sinb, layer, S)
        y_mla = _attention(q.reshape(B, S, -1), k.reshape(B, S, -1), vt)
        y_pool = _pool(proj3, pool_w_b, pool_sc3, layer)
        merged = _merge(y_ret.reshape(T, -1), y_mla.reshape(T, -1), y_pool.reshape(T, -1), gates, w_branch_b, layer)
        xt = _outproj(merged, w_out_b, xt, layer)
        cls, wlo, whi = _router(xt, ln_moe3, w_router_hi, w_router_lo, b_router, layer)
        elo, ehi, n_used, slot_tok, slot_dst, ws_lo, ws_hi = _dispatch_plan(cls, wlo, whi)
        xt = _moe(xt, elo, ehi, n_used, slot_tok, slot_dst, ws_lo, ws_hi, ln_moe3, w_eg, w_eu, w_ed, layer)
        xt = _ple(xt, p3, ln_ple3, w_pg, w_pp, ln_final.reshape(1, D), layer, layer == depth - 1)
    return xt.reshape(B, S, D)
```

```python
import functools
import math

import jax
import jax.numpy as jnp
from jax import lax
from jax.experimental import pallas as pl
from jax.experimental.pallas import tpu as pltpu

F32 = jnp.float32
BF16 = jnp.bfloat16

D_MODEL = 2048
RET_HEADS = 8
RET_HEAD_DIM = 128
RET_WIDTH = RET_HEADS * RET_HEAD_DIM
RET_CHUNK = 128
MLA_HEADS = 8
MLA_Q_LORA = 512
MLA_KV_LORA = 256
MLA_NOPE = 128
MLA_ROPE = 64
MLA_V = 128
MLA_QK_PAD = 256
POOL_WINDOWS = (2, 4, 8, 16)
POOL_GROUP = 256
POOL_WIDTH = len(POOL_WINDOWS) * POOL_GROUP
N_BRANCH = 3
N_GROUPS = 8
EXPERTS_PER_GROUP = 4
N_EXPERTS = N_GROUPS * EXPERTS_PER_GROUP
EXPERT_HIDDEN = 512
PAIRS = ((0, 1), (0, 2), (0, 3), (1, 2), (1, 3), (2, 3))
N_CLASSES = N_GROUPS * len(PAIRS)
PLE_DIM = 256
ROPE_BASE = 10000.0
NORM_EPS = 1e-6

COL_RET = 0
COL_CQ = 4096
COL_CKV = COL_CQ + MLA_Q_LORA
COL_KR = COL_CKV + MLA_KV_LORA
COL_POOL = 5120
COL_GATE = 6144
IN_PAD_WIDTH = COL_GATE + N_BRANCH * D_MODEL

V7X_VMEM_LIMIT = 56 * 1024 * 1024

TM_IN, TN_IN = 1024, 2048
TM_MERGE, TN_MERGE = 512, 2048
TM_OUT = 512
TM_UP = 256
TQ = 1024
TK = 512
ATTN_CHUNKS_PER_BODY = 4
MLA_VT_ROWS = MLA_V + 16
RET_UNROLL = 16
POOL_ROWS = 512
POOL_HALO = 128
MOE_BLOCK = 256
TM_PLE = 512


def _cparams(sem):
    return pltpu.CompilerParams(dimension_semantics=sem, vmem_limit_bytes=V7X_VMEM_LIMIT)


def _rms(x, g):
    return x * lax.rsqrt(jnp.mean(x * x, axis=-1, keepdims=True) + NORM_EPS) * g


def _dot(a, b):
    return jnp.dot(a, b, preferred_element_type=F32)


def _dot_nt(a, b):
    return lax.dot_general(a, b, (((1,), (1,)), ((), ())), preferred_element_type=F32)


def _inproj_kernel(x_ref, g_ref, w_ref, b_ref, o_ref, h_scr, *, gated):
    @pl.when(pl.program_id(1) == 0)
    def _():
        h_scr[...] = _rms(x_ref[...], g_ref[...]).astype(BF16)

    acc = _dot(h_scr[...], w_ref[...])
    if gated:
        acc = jax.nn.sigmoid(acc + b_ref[...])
    o_ref[...] = acc.astype(o_ref.dtype)


def _inproj(x, ln, w, b, layer, col0, ncols, gated):
    T, D = x.shape
    tm, tn = TM_IN, TN_IN
    assert col0 % tn == 0 and ncols % tn == 0
    j0 = col0 // tn
    return pl.pallas_call(
        functools.partial(_inproj_kernel, gated=gated),
        grid=(T // tm, ncols // tn),
        in_specs=[
            pl.BlockSpec((tm, D), lambda i, j: (i, 0)),
            pl.BlockSpec((None, 1, D), lambda i, j: (layer, 0, 0)),
            pl.BlockSpec((None, D, tn), lambda i, j: (layer, 0, j0 + j)),
            pl.BlockSpec((None, 1, tn), lambda i, j: (layer, 0, j0 + j)),
        ],
        out_specs=pl.BlockSpec((tm, tn), lambda i, j: (i, j)),
        out_shape=jax.ShapeDtypeStruct((T, ncols), BF16),
        scratch_shapes=[pltpu.VMEM((tm, D), BF16)],
        compiler_params=_cparams(("parallel", "arbitrary")),
        name="inproj_gate" if gated else "inproj",
    )(x, ln, w, b)


def _ret_kernel(q_ref, k_ref, v_ref, g_ref, cos_ref, sin_ref, lg_ref, gain_ref, o_ref,
                acc_scr, qr_scr, kr_scr, *, seq):
    C = RET_CHUNK
    n_chunks = seq // C
    lg = jax.nn.log_sigmoid(lg_ref[...])
    lgf, lgb = lg[0:1, :], lg[1:2, :]
    ii = lax.broadcasted_iota(jnp.int32, (C, C), 0).astype(F32)
    ll = lax.broadcasted_iota(jnp.int32, (C, C), 1).astype(F32)
    diff = ii - ll
    decay = jnp.where(diff >= 0, jnp.exp(lgf * jnp.maximum(diff, 0.0)),
                      jnp.exp(lgb * jnp.maximum(-diff, 0.0)))
    xi_f = jnp.exp(lgf * (ii + 1.0))
    xi_b = jnp.exp(lgb * (C - ii))
    zeta_f = jnp.exp(lgf * (C - 1.0 - ii))
    zeta_b = jnp.exp(lgb * ii)
    cdec_f = jnp.exp(lgf * C)
    cdec_b = jnp.exp(lgb * C)
    k_scale = RET_HEAD_DIM ** -0.5

    def rope(x, rows):
        return x * cos_ref[rows, :] + pltpu.roll(x, RET_HEAD_DIM // 2, 1) * sin_ref[rows, :]

    def fwd(c, state):
        rows = pl.ds(pl.multiple_of(c * C, C), C)
        q = rope(q_ref[rows, :].astype(F32), rows)
        k = rope(k_ref[rows, :].astype(F32), rows) * k_scale
        qb, kb = q.astype(BF16), k.astype(BF16)
        qr_scr[rows, :] = qb
        kr_scr[rows, :] = kb
        v = v_ref[rows, :]
        s = _dot_nt(qb, kb) * decay
        inner = _dot(s.astype(BF16), v)
        cross = _dot(qb, state.astype(BF16)) * xi_f
        acc_scr[rows, :] = inner + cross
        kv = _dot((k * zeta_f).T.astype(BF16), v)
        return state * cdec_f + kv

    lax.fori_loop(0, n_chunks, fwd, jnp.zeros((RET_HEAD_DIM, RET_HEAD_DIM), F32), unroll=RET_UNROLL)

    def bwd(t, state):
        c = n_chunks - 1 - t
        rows = pl.ds(pl.multiple_of(c * C, C), C)
        qb = qr_scr[rows, :]
        v = v_ref[rows, :]
        y = acc_scr[rows, :] + _dot(qb, state.astype(BF16)) * xi_b
        yn = _rms(y, gain_ref[...])
        g = g_ref[rows, :].astype(F32)
        o_ref[rows, :] = (yn * (g * jax.nn.sigmoid(g))).astype(o_ref.dtype)
        kv = _dot((kr_scr[rows, :].astype(F32) * zeta_b).T.astype(BF16), v)
        return state * cdec_b + kv

    lax.fori_loop(0, n_chunks, bwd, jnp.zeros((RET_HEAD_DIM, RET_HEAD_DIM), F32), unroll=RET_UNROLL)


def _retention(proj, cos, sin, decay_logit, gain, layer):
    B, S, _ = proj.shape
    hd = RET_HEAD_DIM
    nh = RET_HEADS

    def col(part):
        return pl.BlockSpec((None, S, hd), lambda b, h: (b, 0, COL_RET // hd + part * nh + h))

    return pl.pallas_call(
        functools.partial(_ret_kernel, seq=S),
        grid=(B, nh),
        in_specs=[
            col(0), col(1), col(2), col(3),
            pl.BlockSpec((S, hd), lambda b, h: (0, 0)),
            pl.BlockSpec((S, hd), lambda b, h: (0, 0)),
            pl.BlockSpec((None, None, 2, hd), lambda b, h: (layer, h, 0, 0)),
            pl.BlockSpec((None, None, 1, hd), lambda b, h: (layer, h, 0, 0)),
        ],
        out_specs=pl.BlockSpec((None, S, hd), lambda b, h: (b, 0, h)),
        out_shape=jax.ShapeDtypeStruct((B, S, RET_WIDTH), BF16),
        scratch_shapes=[pltpu.VMEM((S, hd), F32), pltpu.VMEM((S, hd), BF16), pltpu.VMEM((S, hd), BF16)],
        compiler_params=_cparams(("parallel", "arbitrary")),
        name="retention",
    )(proj, proj, proj, proj, cos, sin, decay_logit, gain)


def _mla_up_kernel(cq_ref, ckv_ref, kr_ref, qn_ref, kvn_ref, wq_ref, wkv_ref, cos_ref, sina_ref, sinb_ref,
                   q_out, k_out, vt_out):
    scale = (MLA_NOPE + MLA_ROPE) ** -0.5 * math.log2(math.e)
    cosp, sina, sinb = cos_ref[...], sina_ref[...], sinb_ref[...]

    def rope(x):
        return x * cosp + pltpu.roll(x, MLA_ROPE // 2, 1) * sina + pltpu.roll(x, 128 - MLA_ROPE // 2, 1) * sinb

    q = _dot(_rms(cq_ref[...].astype(F32), qn_ref[...]).astype(BF16), wq_ref[...])
    kv = _dot(_rms(ckv_ref[...].astype(F32), kvn_ref[...]).astype(BF16), wkv_ref[...])
    k_rope = rope(kr_ref[...].astype(F32)).astype(BF16)
    for h in range(MLA_HEADS):
        lo = h * MLA_QK_PAD
        q_out[:, lo:lo + MLA_NOPE] = (q[:, lo:lo + MLA_NOPE] * scale).astype(BF16)
        q_out[:, lo + MLA_NOPE:lo + MLA_QK_PAD] = (rope(q[:, lo + MLA_NOPE:lo + MLA_QK_PAD]) * scale).astype(BF16)
        k_out[:, lo:lo + MLA_NOPE] = kv[:, h * MLA_NOPE:(h + 1) * MLA_NOPE].astype(BF16)
        k_out[:, lo + MLA_NOPE:lo + MLA_QK_PAD] = k_rope
    v_t = kv[:, MLA_HEADS * MLA_NOPE:].T
    ones = jnp.ones((MLA_VT_ROWS - MLA_V, v_t.shape[1]), BF16)
    for h in range(MLA_HEADS):
        lo = h * MLA_VT_ROWS
        vt_out[lo:lo + MLA_V, :] = v_t[h * MLA_V:(h + 1) * MLA_V, :].astype(BF16)
        vt_out[lo + MLA_V:lo + MLA_VT_ROWS, :] = ones


def _mla_up(proj2d, qn, kvn, wq, wkv, cosp, sina, sinb, layer, seq):
    T = proj2d.shape[0]
    tm = TM_UP
    n_s = seq // tm
    qk_w = MLA_HEADS * MLA_QK_PAD
    kv_w = MLA_HEADS * (MLA_NOPE + MLA_V)
    tab = pl.BlockSpec((tm, 128), lambda i: (i % n_s, 0))
    return pl.pallas_call(
        _mla_up_kernel,
        grid=(T // tm,),
        in_specs=[
            pl.BlockSpec((tm, MLA_Q_LORA), lambda i: (i, COL_CQ // MLA_Q_LORA)),
            pl.BlockSpec((tm, MLA_KV_LORA), lambda i: (i, COL_CKV // MLA_KV_LORA)),
            pl.BlockSpec((tm, 128), lambda i: (i, COL_KR // 128)),
            pl.BlockSpec((None, 1, MLA_Q_LORA), lambda i: (layer, 0, 0)),
            pl.BlockSpec((None, 1, MLA_KV_LORA), lambda i: (layer, 0, 0)),
            pl.BlockSpec((None, MLA_Q_LORA, qk_w), lambda i: (layer, 0, 0)),
            pl.BlockSpec((None, MLA_KV_LORA, kv_w), lambda i: (layer, 0, 0)),
            tab, tab, tab,
        ],
        out_specs=[
            pl.BlockSpec((tm, qk_w), lambda i: (i, 0)),
            pl.BlockSpec((tm, qk_w), lambda i: (i, 0)),
            pl.BlockSpec((None, MLA_HEADS * MLA_VT_ROWS, tm), lambda i: (i // n_s, 0, i % n_s)),
        ],
        out_shape=[
            jax.ShapeDtypeStruct((T, qk_w), BF16),
            jax.ShapeDtypeStruct((T, qk_w), BF16),
            jax.ShapeDtypeStruct((T // seq, MLA_HEADS * MLA_VT_ROWS, seq), BF16),
        ],
        compiler_params=_cparams(("parallel",)),
        name="mla_up",
    )(proj2d, proj2d, proj2d, qn, kvn, wq, wkv, cosp, sina, sinb)


def _attn_kernel(q_ref, k_ref, vt_ref, o_ref, acc_scr, st_a, st_b, *, seq):
    n = seq // TK
    U = ATTN_CHUNKS_PER_BODY
    q = q_ref[...]
    acc_scr[...] = jnp.zeros(acc_scr.shape, F32)
    bufs = (st_a, st_b)

    def scores(c, dst):
        r0 = pl.multiple_of(c * TK, TK)
        dst[...] = _dot_nt(k_ref[pl.ds(r0, TK), :], q)

    def absorb(c, src, m):
        r0 = pl.multiple_of(c * TK, TK)
        st = src[...]
        m_new = jnp.maximum(m, jnp.max(st, axis=0, keepdims=True))
        alpha = jnp.exp2(m - m_new)
        pt = jnp.exp2(st - m_new).astype(BF16)
        acc_scr[...] = alpha * acc_scr[...] + _dot(vt_ref[:, pl.ds(r0, TK)], pt)
        return m_new

    scores(0, st_a)

    def body(jj, m):
        for u in range(U):
            scores(U * jj + u + 1, bufs[(u + 1) % 2])
            m = absorb(U * jj + u, bufs[u % 2], m)
        return m

    m = lax.fori_loop(0, n // U - 1, body, jnp.full((1, TQ), -jnp.inf, F32))
    for u in range(U):
        c = n - U + u
        if u + 1 < U:
            scores(c + 1, bufs[(u + 1) % 2])
        m = absorb(c, bufs[u % 2], m)
    acc = acc_scr[...]
    out_t = acc[0:MLA_V, :] / acc[MLA_V:MLA_V + 1, :]
    o_ref[...] = out_t.T.astype(o_ref.dtype)


def _attention(q, k, vt):
    B, S, _ = q.shape
    assert ATTN_CHUNKS_PER_BODY % 2 == 0 and (S // TK) % ATTN_CHUNKS_PER_BODY == 0
    return pl.pallas_call(
        functools.partial(_attn_kernel, seq=S),
        grid=(B, MLA_HEADS, S // TQ),
        in_specs=[
            pl.BlockSpec((None, TQ, MLA_QK_PAD), lambda b, h, i: (b, i, h)),
            pl.BlockSpec((None, S, MLA_QK_PAD), lambda b, h, i: (b, 0, h)),
            pl.BlockSpec((None, MLA_VT_ROWS, S), lambda b, h, i: (b, h, 0)),
        ],
        out_specs=pl.BlockSpec((None, TQ, MLA_V), lambda b, h, i: (b, i, h)),
        out_shape=jax.ShapeDtypeStruct((B, S, MLA_HEADS * MLA_V), BF16),
        scratch_shapes=[pltpu.VMEM((MLA_VT_ROWS, TQ), F32), pltpu.VMEM((TK, TQ), F32), pltpu.VMEM((TK, TQ), F32)],
        compiler_params=_cparams(("parallel", "parallel", "arbitrary")),
        name="attention",
    )(q, k, vt)


def _pool_kernel(u_ref, w_ref, sc_ref, o_ref, pad_scr, *, seq):
    gi = pl.program_id(1)
    half = jnp.int32(0)
    for k, w in enumerate(POOL_WINDOWS):
        half = jnp.where(gi == k, w // 2, half)
    R, H = POOL_ROWS, POOL_HALO
    pad_scr[0:H, :] = jnp.zeros((H, POOL_GROUP), BF16)
    pad_scr[H + seq:H + seq + H, :] = jnp.zeros((H, POOL_GROUP), BF16)
    pad_scr[H:H + seq, :] = u_ref[...]
    ii = lax.broadcasted_iota(jnp.int32, (R, R + 2 * H), 0)
    jj = lax.broadcasted_iota(jnp.int32, (R, R + 2 * H), 1)
    band = ((jj >= ii + H - half) & (jj < ii + H + half)).astype(BF16)
    rr = lax.broadcasted_iota(jnp.int32, (R, 1), 0)

    def body(t, carry):
        r0 = pl.multiple_of(t * R, R)
        total = _dot(band, pad_scr[pl.ds(r0, R + 2 * H), :])
        pos = r0 + rr
        cnt = (jnp.minimum(pos + half, seq) - jnp.maximum(pos - half, 0)).astype(F32)
        mixed = total / cnt - u_ref[pl.ds(r0, R), :].astype(F32)
        y = _dot(mixed.astype(BF16), w_ref[...]) * sc_ref[...]
        o_ref[pl.ds(r0, R), :] = y.astype(o_ref.dtype)
        return carry

    lax.fori_loop(0, seq // R, body, 0)


def _pool(proj, pool_w, pool_scale, layer):
    B, S, _ = proj.shape
    ng = len(POOL_WINDOWS)
    return pl.pallas_call(
        functools.partial(_pool_kernel, seq=S),
        grid=(B, ng),
        in_specs=[
            pl.BlockSpec((None, S, POOL_GROUP), lambda b, g: (b, 0, COL_POOL // POOL_GROUP + g)),
            pl.BlockSpec((None, None, POOL_GROUP, POOL_GROUP), lambda b, g: (layer, g, 0, 0)),
            pl.BlockSpec((None, 1, POOL_GROUP), lambda b, g: (layer, 0, g)),
        ],
        out_specs=pl.BlockSpec((None, S, POOL_GROUP), lambda b, g: (b, 0, g)),
        out_shape=jax.ShapeDtypeStruct((B, S, POOL_WIDTH), BF16),
        scratch_shapes=[pltpu.VMEM((S + 2 * POOL_HALO, POOL_GROUP), BF16)],
        compiler_params=_cparams(("parallel", "arbitrary")),
        name="pool",
    )(proj, pool_w, pool_scale)


def _merge_kernel(y0_ref, y1_ref, y2_ref, g0_ref, g1_ref, g2_ref, w0_ref, w1_ref, w2_ref, o_ref):
    acc = g0_ref[...].astype(F32) * _dot(y0_ref[...], w0_ref[...])
    acc += g1_ref[...].astype(F32) * _dot(y1_ref[...], w1_ref[...])
    acc += g2_ref[...].astype(F32) * _dot(y2_ref[...], w2_ref[...])
    o_ref[...] = acc.astype(o_ref.dtype)


def _merge(y_ret, y_mla, y_pool, gates, w_branch, layer):
    T, W = y_ret.shape
    D = D_MODEL
    tm, tn = TM_MERGE, TN_MERGE
    ysp = pl.BlockSpec((tm, W), lambda i, j: (i, 0))

    def gate(b):
        return pl.BlockSpec((tm, tn), lambda i, j: (i, b * D // tn + j))

    def wsp(b):
        return pl.BlockSpec((None, None, W, tn), lambda i, j: (layer, b, 0, j))

    return pl.pallas_call(
        _merge_kernel,
        grid=(T // tm, D // tn),
        in_specs=[ysp, ysp, ysp, gate(0), gate(1), gate(2), wsp(0), wsp(1), wsp(2)],
        out_specs=pl.BlockSpec((tm, tn), lambda i, j: (i, j)),
        out_shape=jax.ShapeDtypeStruct((T, D), BF16),
        compiler_params=_cparams(("parallel", "arbitrary")),
        name="merge",
    )(y_ret, y_mla, y_pool, gates, gates, gates, w_branch, w_branch, w_branch)


def _outproj_router_kernel(m_ref, w_ref, x_ref, ln_ref, w_hi_ref, w_lo_ref, b_ref, o_ref, cls_ref, wlo_ref, whi_ref):
    x = x_ref[...] + _dot(m_ref[...], w_ref[...])
    o_ref[...] = x
    h = _rms(x, ln_ref[...])
    h_hi = h.astype(BF16)
    h_lo = (h - h_hi.astype(F32)).astype(BF16)
    lt = _dot_nt(w_hi_ref[...], h_hi) + _dot_nt(w_hi_ref[...], h_lo) + _dot_nt(w_lo_ref[...], h_hi)
    lt = lt + b_ref[...]
    grp = [lt[g:g + 1, :] for g in range(N_GROUPS)]
    gmax = functools.reduce(jnp.maximum, grp)
    gden = functools.reduce(lambda a, b: a + b, [jnp.exp(x - gmax) for x in grp])
    g_p = 1.0 / gden
    gidx = jnp.full(gmax.shape, N_GROUPS, jnp.int32)
    for g in reversed(range(N_GROUPS)):
        gidx = jnp.where(grp[g] == gmax, g, gidx)
    sel = []
    for j in range(EXPERTS_PER_GROUP):
        e = jnp.zeros_like(gmax)
        for g in range(N_GROUPS):
            r = N_GROUPS + g * EXPERTS_PER_GROUP + j
            e = jnp.where(gidx == g, lt[r:r + 1, :], e)
        sel.append(e)
    emax = functools.reduce(jnp.maximum, sel)
    ex = [jnp.exp(e - emax) for e in sel]
    eden = functools.reduce(lambda a, b: a + b, ex)
    pe = [e / eden for e in ex]

    def top1(vals):
        best = functools.reduce(jnp.maximum, vals)
        idx = jnp.full(best.shape, len(vals), jnp.int32)
        for j in reversed(range(len(vals))):
            idx = jnp.where(vals[j] == best, j, idx)
        return best, idx

    p1, i1 = top1(pe)
    p2, i2 = top1([jnp.where(i1 == j, -1.0, pe[j]) for j in range(EXPERTS_PER_GROUP)])
    den = p1 + p2
    w1 = g_p * (p1 / den)
    w2 = g_p * (p2 / den)
    lo = jnp.minimum(i1, i2)
    hi = jnp.maximum(i1, i2)
    pair = jnp.zeros_like(lo)
    for n, (a, b) in enumerate(PAIRS):
        pair = jnp.where((lo == a) & (hi == b), n, pair)
    cls_ref[...] = gidx * len(PAIRS) + pair
    first_is_lo = i1 < i2
    wlo_ref[...] = jnp.where(first_is_lo, w1, w2)
    whi_ref[...] = jnp.where(first_is_lo, w2, w1)


def _outproj_router(merged, w_out, x, ln, w_hi, w_lo, bias, layer):
    T, D = x.shape
    tm = TM_OUT
    n_log = N_GROUPS + N_EXPERTS
    rows = pl.BlockSpec((tm, D), lambda i: (i, 0))
    row = pl.BlockSpec((None, 1, tm), lambda i: (i, 0, 0))
    outs = pl.pallas_call(
        _outproj_router_kernel,
        grid=(T // tm,),
        in_specs=[
            rows,
            pl.BlockSpec((None, D, D), lambda i: (layer, 0, 0)),
            rows,
            pl.BlockSpec((None, 1, D), lambda i: (layer, 0, 0)),
            pl.BlockSpec((None, n_log, D), lambda i: (layer, 0, 0)),
            pl.BlockSpec((None, n_log, D), lambda i: (layer, 0, 0)),
            pl.BlockSpec((None, n_log, 1), lambda i: (layer, 0, 0)),
        ],
        out_specs=[rows, row, row, row],
        out_shape=[
            jax.ShapeDtypeStruct((T, D), F32),
            jax.ShapeDtypeStruct((T // tm, 1, tm), jnp.int32),
            jax.ShapeDtypeStruct((T // tm, 1, tm), F32),
            jax.ShapeDtypeStruct((T // tm, 1, tm), F32),
        ],
        compiler_params=_cparams(("parallel",)),
        name="outproj_router",
    )(merged, w_out, x, ln, w_hi, w_lo, bias)
    return outs[0], outs[1].reshape(T), outs[2].reshape(T), outs[3].reshape(T)


def _moe_kernel(elo_ref, ehi_ref, nused_ref, tok_ref, dst_ref,
                x_hbm, ws_ref, ln_ref, wg_lo, wu_lo, wd_lo, wg_hi, wu_hi, wd_hi,
                o_hbm, xbuf, obuf, gsem, ssem):
    del elo_ref, ehi_ref
    bm = MOE_BLOCK
    i = pl.program_id(0)
    n_used = nused_ref[0]
    cur = i % 2

    def start_gather(block, buf):
        for r in range(bm):
            pltpu.make_async_copy(x_hbm.at[tok_ref[block * bm + r]], xbuf.at[buf, r], gsem.at[buf]).start()

    def wait_gather(buf):
        pltpu.make_async_copy(x_hbm.at[pl.ds(0, bm)], xbuf.at[buf], gsem.at[buf]).wait()

    def start_scatter(block, buf):
        for r in range(bm):
            pltpu.make_async_copy(obuf.at[buf, r], o_hbm.at[dst_ref[(block + 1) * bm + r]], ssem).start()

    def wait_scatter():
        pltpu.make_async_copy(obuf.at[0], o_hbm.at[pl.ds(0, bm)], ssem).wait()

    @pl.when(i == 0)
    def _():
        obuf[...] = jnp.zeros(obuf.shape, F32)
        start_gather(0, 0)
        start_scatter(-1, 1)

    @pl.when(i < n_used)
    def _():
        wait_gather(cur)
        wait_scatter()
        start_gather(jnp.minimum(i + 1, n_used - 1), 1 - cur)
        start_scatter(i - 1, 1 - cur)
        x = xbuf[cur]
        h = _rms(x, ln_ref[...]).astype(BF16)

        def ffn(wg, wu, wd):
            a = _dot(h, wg[...])
            hidden = (a * jax.nn.sigmoid(a)) * _dot(h, wu[...])
            return _dot(hidden.astype(BF16), wd[...])

        ws = ws_ref[...]
        y = ws[:, 0:1] * ffn(wg_lo, wu_lo, wd_lo) + ws[:, 1:2] * ffn(wg_hi, wu_hi, wd_hi)
        obuf[cur] = x + y

        @pl.when(i == n_used - 1)
        def _():
            wait_scatter()
            start_scatter(i, cur)
            wait_scatter()
            wait_gather(1 - cur)


def _moe(x, elo, ehi, n_used, slot_tok, slot_dst, ws, ln, w_gate, w_up, w_down, layer):
    T, D = x.shape
    bm = MOE_BLOCK
    n_blocks = elo.shape[0]
    Hd = EXPERT_HIDDEN

    def wspec(shape, which):
        def imap(i, elo_r, ehi_r, *_):
            e = elo_r[i] if which == 0 else ehi_r[i]
            return (layer, e, 0, 0)
        return pl.BlockSpec((None, None) + shape, imap)

    col = pl.BlockSpec((bm, 2), lambda i, *_: (i, 0))
    grid_spec = pltpu.PrefetchScalarGridSpec(
        num_scalar_prefetch=5,
        grid=(n_blocks,),
        in_specs=[
            pl.BlockSpec(memory_space=pl.ANY),
            col,
            pl.BlockSpec((None, 1, D), lambda i, *_: (layer, 0, 0)),
            wspec((D, Hd), 0), wspec((D, Hd), 0), wspec((Hd, D), 0),
            wspec((D, Hd), 1), wspec((D, Hd), 1), wspec((Hd, D), 1),
        ],
        out_specs=pl.BlockSpec(memory_space=pl.ANY),
        scratch_shapes=[
            pltpu.VMEM((2, bm, D), F32), pltpu.VMEM((2, bm, D), F32),
            pltpu.SemaphoreType.DMA((2,)), pltpu.SemaphoreType.DMA,
        ],
    )
    return pl.pallas_call(
        _moe_kernel,
        grid_spec=grid_spec,
        out_shape=jax.ShapeDtypeStruct((T + bm, D), F32),
        compiler_params=_cparams(("arbitrary",)),
        name="moe",
    )(elo, ehi, n_used, slot_tok, slot_dst, x, ws, ln, w_gate, w_up, w_down, w_gate, w_up, w_down)


def _dispatch_plan(cls, wlo, whi):
    T = cls.shape[0]
    bm = MOE_BLOCK
    n_blocks = (T + N_CLASSES * (bm - 1)) // bm
    n_slots = n_blocks * bm
    order = jnp.argsort(cls).astype(jnp.int32)
    counts = jnp.sum(cls[None, :] == jnp.arange(N_CLASSES, dtype=jnp.int32)[:, None], axis=1, dtype=jnp.int32)
    padded = (counts + bm - 1) // bm * bm
    pends = jnp.cumsum(padded)
    pstarts = pends - padded
    starts = jnp.cumsum(counts) - counts
    n_used = pends[-1] // bm
    blk = jnp.arange(n_blocks, dtype=jnp.int32)
    bidx = jnp.minimum(blk, n_used - 1)
    bcls = jnp.minimum(jnp.sum(pends[None, :] <= (bidx * bm)[:, None], axis=1, dtype=jnp.int32), N_CLASSES - 1)
    onehot = bcls[:, None] == jnp.arange(N_CLASSES, dtype=jnp.int32)[None, :]

    def per_block(table):
        return jnp.sum(jnp.where(onehot, table[None, :], 0), axis=1, dtype=jnp.int32)

    off = bidx * bm - per_block(pstarts)
    nval = jnp.where(blk < n_used, jnp.clip(per_block(counts) - off, 0, bm), 0)
    r = jnp.arange(bm, dtype=jnp.int32)[None, :]
    valid = r < nval[:, None]
    src = jnp.clip((per_block(starts) + off)[:, None] + r, 0, T - 1)
    tok2d = jnp.where(valid, order[src], 0)
    dst2d = jnp.where(valid, tok2d, T + r)
    slot_tok = tok2d.reshape(n_slots)
    slot_dst = jnp.concatenate([T + r[0], dst2d.reshape(n_slots)])
    grp = bcls // len(PAIRS)
    pair = bcls % len(PAIRS)
    pair_lo, pair_hi = jnp.zeros_like(pair), jnp.zeros_like(pair)
    for n, (a, b) in enumerate(PAIRS):
        pair_lo = jnp.where(pair == n, a, pair_lo)
        pair_hi = jnp.where(pair == n, b, pair_hi)
    elo = grp * EXPERTS_PER_GROUP + pair_lo
    ehi = grp * EXPERTS_PER_GROUP + pair_hi
    ws = jnp.stack([wlo[slot_tok], whi[slot_tok]], axis=-1)
    return elo, ehi, n_used.reshape(1).astype(jnp.int32), slot_tok, slot_dst, ws


def _ple_kernel(x_ref, p_ref, ln_ref, wg_ref, wp_ref, fin_ref, o_ref, *, final_norm):
    x = x_ref[...]
    gate = jax.nn.sigmoid(_dot(_rms(x, ln_ref[...]).astype(BF16), wg_ref[...]))
    y = x + gate * _dot(p_ref[...].astype(BF16), wp_ref[...])
    o_ref[...] = _rms(y, fin_ref[...]) if final_norm else y


def _ple(x, p, ln, w_gate, w_proj, fin_gain, layer, final_norm):
    T, D = p.shape[1], x.shape[1]
    return pl.pallas_call(
        functools.partial(_ple_kernel, final_norm=final_norm),
        grid=(T // TM_PLE,),
        in_specs=[
            pl.BlockSpec((TM_PLE, D), lambda i: (i, 0)),
            pl.BlockSpec((None, TM_PLE, PLE_DIM), lambda i: (layer, i, 0)),
            pl.BlockSpec((None, 1, D), lambda i: (layer, 0, 0)),
            pl.BlockSpec((None, D, D), lambda i: (layer, 0, 0)),
            pl.BlockSpec((None, PLE_DIM, D), lambda i: (layer, 0, 0)),
            pl.BlockSpec((1, D), lambda i: (0, 0)),
        ],
        out_specs=pl.BlockSpec((TM_PLE, D), lambda i: (i, 0)),
        out_shape=jax.ShapeDtypeStruct((T, D), F32),
        compiler_params=_cparams(("parallel",)),
        name="ple_final" if final_norm else "ple",
    )(x, p, ln, w_gate, w_proj, fin_gain)


def _rope_tables(seq, dim):
    inv = 1.0 / (ROPE_BASE ** (jnp.arange(0, dim, 2, dtype=F32) / dim))
    ang = jnp.arange(seq, dtype=F32)[:, None] * inv[None, :]
    return jnp.cos(ang), jnp.sin(ang)


def _prep_in_weights(w_in, b_gate):
    depth, D, _ = w_in.shape
    ret_w = 4 * RET_WIDTH
    mla_w = MLA_Q_LORA + MLA_KV_LORA + MLA_ROPE
    pad = COL_POOL - COL_KR - MLA_ROPE
    w = jnp.concatenate([
        w_in[..., :ret_w + mla_w],
        jnp.zeros((depth, D, pad), w_in.dtype),
        w_in[..., ret_w + mla_w:],
    ], axis=-1).astype(BF16)
    b = jnp.concatenate([jnp.zeros((depth, COL_GATE), F32), b_gate], axis=-1).reshape(depth, 1, IN_PAD_WIDTH)
    return w, b


def _prep_mla_weights(w_uq, w_ukv):
    depth = w_uq.shape[0]
    wq = w_uq.reshape(depth, MLA_Q_LORA, MLA_HEADS, MLA_NOPE + MLA_ROPE)
    wq = jnp.pad(wq, ((0, 0), (0, 0), (0, 0), (0, MLA_QK_PAD - MLA_NOPE - MLA_ROPE)))
    wq = wq.reshape(depth, MLA_Q_LORA, MLA_HEADS * MLA_QK_PAD).astype(BF16)
    wkv = w_ukv.reshape(depth, MLA_KV_LORA, MLA_HEADS, MLA_NOPE + MLA_V)
    wkv = jnp.concatenate([
        wkv[..., :MLA_NOPE].reshape(depth, MLA_KV_LORA, MLA_HEADS * MLA_NOPE),
        wkv[..., MLA_NOPE:].reshape(depth, MLA_KV_LORA, MLA_HEADS * MLA_V),
    ], axis=-1).astype(BF16)
    return wq, wkv


def _split_bf16(w):
    hi = w.astype(BF16)
    lo = (w - hi.astype(F32)).astype(BF16)
    return hi, lo


def kernel(x, p, ln_mix, w_in, b_gate, ret_decay_logit, ret_norm, mla_q_norm, mla_w_uq, mla_kv_norm, mla_w_ukv, pool_w, pool_scale, w_branch, w_out, ln_moe, w_grp, b_grp, w_rt, b_rt, w_exp_gate, w_exp_up, w_exp_down, ln_ple, w_ple_gate, w_ple_proj, ln_final):
    B, S, D = x.shape
    depth = p.shape[0]
    T = B * S
    assert D == D_MODEL and S % max(TQ, TK, POOL_ROWS, TM_UP) == 0
    assert all(T % t == 0 for t in (TM_IN, TM_MERGE, TM_OUT, TM_PLE, MOE_BLOCK))

    cos_r, sin_r = _rope_tables(S, RET_HEAD_DIM)
    ret_cos = jnp.concatenate([cos_r, cos_r], axis=-1)
    ret_sin = jnp.concatenate([-sin_r, sin_r], axis=-1)
    cos_m, sin_m = _rope_tables(S, MLA_ROPE)
    z32 = jnp.zeros_like(sin_m)
    z64 = jnp.zeros((S, 128 - MLA_ROPE), F32)
    mla_cos = jnp.concatenate([cos_m, cos_m, z64], axis=-1)
    mla_sina = jnp.concatenate([z32, sin_m, z64], axis=-1)
    mla_sinb = jnp.concatenate([-sin_m, z32, z64], axis=-1)

    w_in_p, b_in = _prep_in_weights(w_in, b_gate)
    wq, wkv = _prep_mla_weights(mla_w_uq, mla_w_ukv)
    ln_mix3 = ln_mix.reshape(depth, 1, D)
    ln_moe3 = ln_moe.reshape(depth, 1, D)
    ln_ple3 = ln_ple.reshape(depth, 1, D)
    qn3 = mla_q_norm.reshape(depth, 1, MLA_Q_LORA)
    kvn3 = mla_kv_norm.reshape(depth, 1, MLA_KV_LORA)
    decay4 = jnp.broadcast_to(ret_decay_logit.transpose(0, 2, 1)[..., None], (depth, RET_HEADS, 2, RET_HEAD_DIM))
    gain4 = ret_norm.reshape(depth, RET_HEADS, 1, RET_HEAD_DIM)
    pool_w_b = pool_w.astype(BF16)
    pool_sc3 = pool_scale.reshape(depth, 1, POOL_WIDTH)
    w_branch_b = w_branch.astype(BF16)
    w_out_b = w_out.astype(BF16)
    w_router = jnp.concatenate([w_grp, w_rt], axis=-1).transpose(0, 2, 1)
    w_router_hi, w_router_lo = _split_bf16(w_router)
    b_router = jnp.concatenate([b_grp, b_rt], axis=-1).reshape(depth, N_GROUPS + N_EXPERTS, 1)
    w_eg = w_exp_gate.astype(BF16)
    w_eu = w_exp_up.astype(BF16)
    w_ed = w_exp_down.astype(BF16)
    w_pg = w_ple_gate.astype(BF16)
    w_pp = w_ple_proj.astype(BF16)
    p3 = p.reshape(depth, T, PLE_DIM)

    xt = x.reshape(T, D)
    for layer in range(depth):
        proj = _inproj(xt, ln_mix3, w_in_p, b_in, layer, 0, COL_GATE, False)
        gates = _inproj(xt, ln_mix3, w_in_p, b_in, layer, COL_GATE, N_BRANCH * D, True)
        proj3 = proj.reshape(B, S, COL_GATE)
        y_ret = _retention(proj3, ret_cos, ret_sin, decay4, gain4, layer)
        q, k, vt = _mla_up(proj, qn3, kvn3, wq, wkv, mla_cos, mla_sina, mla_sinb, layer, S)
        y_mla = _attention(q.reshape(B, S, -1), k.reshape(B, S, -1), vt)
        y_pool = _pool(proj3, pool_w_b, pool_sc3, layer)
        merged = _merge(y_ret.reshape(T, -1), y_mla.reshape(T, -1), y_pool.reshape(T, -1), gates, w_branch_b, layer)
        xt, cls, wlo, whi = _outproj_router(merged, w_out_b, xt, ln_moe3, w_router_hi, w_router_lo, b_router, layer)
        elo, ehi, n_used, slot_tok, slot_dst, ws = _dispatch_plan(cls, wlo, whi)
        xt = _moe(xt, elo, ehi, n_used, slot_tok, slot_dst, ws, ln_moe3, w_eg, w_eu, w_ed, layer)
        xt = _ple(xt, p3, ln_ple3, w_pg, w_pp, ln_final.reshape(1, D), layer, layer == depth - 1)
    return xt.reshape(B, S, D)
```

```python
import functools
import math

import jax
import jax.numpy as jnp
from jax import lax
from jax.experimental import pallas as pl
from jax.experimental.pallas import tpu as pltpu

F32 = jnp.float32
BF16 = jnp.bfloat16

D_MODEL = 2048
RET_HEADS = 8
RET_HEAD_DIM = 128
RET_WIDTH = RET_HEADS * RET_HEAD_DIM
RET_CHUNK = 128
MLA_HEADS = 8
MLA_Q_LORA = 512
MLA_KV_LORA = 256
MLA_NOPE = 128
MLA_ROPE = 64
MLA_V = 128
MLA_QK_PAD = 256
POOL_WINDOWS = (2, 4, 8, 16)
POOL_GROUP = 256
POOL_WIDTH = len(POOL_WINDOWS) * POOL_GROUP
N_BRANCH = 3
N_GROUPS = 8
EXPERTS_PER_GROUP = 4
N_EXPERTS = N_GROUPS * EXPERTS_PER_GROUP
EXPERT_HIDDEN = 512
PAIRS = ((0, 1), (0, 2), (0, 3), (1, 2), (1, 3), (2, 3))
N_CLASSES = N_GROUPS * len(PAIRS)
PLE_DIM = 256
ROPE_BASE = 10000.0
NORM_EPS = 1e-6

COL_RET = 0
COL_CQ = 4096
COL_CKV = COL_CQ + MLA_Q_LORA
COL_KR = COL_CKV + MLA_KV_LORA
COL_POOL = 5120
COL_GATE = 6144
IN_PAD_WIDTH = COL_GATE + N_BRANCH * D_MODEL

V7X_VMEM_LIMIT = 56 * 1024 * 1024

TM_IN, TN_IN = 1024, 2048
TM_MERGE, TN_MERGE = 512, 2048
TM_OUT = 512
TQ = 2048
TK = 512
ATTN_CHUNKS_PER_BODY = 2
ATTN_STRIP = 256
TM_UP = 512
MLA_VT_ROWS = MLA_V + 16
RET_UNROLL = 16
POOL_ROWS = 512
POOL_HALO = 128
MOE_BLOCK = 256
TM_PLE = 512


def _cparams(sem):
    return pltpu.CompilerParams(dimension_semantics=sem, vmem_limit_bytes=V7X_VMEM_LIMIT)


def _rms(x, g):
    return x * lax.rsqrt(jnp.mean(x * x, axis=-1, keepdims=True) + NORM_EPS) * g


def _dot(a, b):
    return jnp.dot(a, b, preferred_element_type=F32)


def _dot_nt(a, b):
    return lax.dot_general(a, b, (((1,), (1,)), ((), ())), preferred_element_type=F32)


def _inproj_kernel(x_ref, g_ref, w_ref, b_ref, o_ref, h_scr, *, gated):
    @pl.when(pl.program_id(1) == 0)
    def _():
        h_scr[...] = _rms(x_ref[...], g_ref[...]).astype(BF16)

    acc = _dot(h_scr[...], w_ref[...])
    if gated:
        acc = jax.nn.sigmoid(acc + b_ref[...])
    o_ref[...] = acc.astype(o_ref.dtype)


def _inproj(x, ln, w, b, layer, col0, ncols, gated):
    T, D = x.shape
    tm, tn = TM_IN, TN_IN
    assert col0 % tn == 0 and ncols % tn == 0
    j0 = col0 // tn
    return pl.pallas_call(
        functools.partial(_inproj_kernel, gated=gated),
        grid=(T // tm, ncols // tn),
        in_specs=[
            pl.BlockSpec((tm, D), lambda i, j: (i, 0)),
            pl.BlockSpec((None, 1, D), lambda i, j: (layer, 0, 0)),
            pl.BlockSpec((None, D, tn), lambda i, j: (layer, 0, j0 + j)),
            pl.BlockSpec((None, 1, tn), lambda i, j: (layer, 0, j0 + j)),
        ],
        out_specs=pl.BlockSpec((tm, tn), lambda i, j: (i, j)),
        out_shape=jax.ShapeDtypeStruct((T, ncols), BF16),
        scratch_shapes=[pltpu.VMEM((tm, D), BF16)],
        compiler_params=_cparams(("parallel", "arbitrary")),
        name="inproj_gate" if gated else "inproj",
    )(x, ln, w, b)


def _ret_kernel(q_ref, k_ref, v_ref, g_ref, cos_ref, sin_ref, lg_ref, gain_ref, o_ref,
                acc_scr, qr_scr, kr_scr, *, seq):
    C = RET_CHUNK
    n_chunks = seq // C
    lg = jax.nn.log_sigmoid(lg_ref[...])
    lgf, lgb = lg[0:1, :], lg[1:2, :]
    ii = lax.broadcasted_iota(jnp.int32, (C, C), 0).astype(F32)
    ll = lax.broadcasted_iota(jnp.int32, (C, C), 1).astype(F32)
    diff = ii - ll
    decay = jnp.where(diff >= 0, jnp.exp(lgf * jnp.maximum(diff, 0.0)),
                      jnp.exp(lgb * jnp.maximum(-diff, 0.0)))
    xi_f = jnp.exp(lgf * (ii + 1.0))
    xi_b = jnp.exp(lgb * (C - ii))
    zeta_f = jnp.exp(lgf * (C - 1.0 - ii))
    zeta_b = jnp.exp(lgb * ii)
    cdec_f = jnp.exp(lgf * C)
    cdec_b = jnp.exp(lgb * C)
    k_scale = RET_HEAD_DIM ** -0.5

    def rope(x, rows):
        return x * cos_ref[rows, :] + pltpu.roll(x, RET_HEAD_DIM // 2, 1) * sin_ref[rows, :]

    def fwd(c, state):
        rows = pl.ds(pl.multiple_of(c * C, C), C)
        q = rope(q_ref[rows, :].astype(F32), rows)
        k = rope(k_ref[rows, :].astype(F32), rows) * k_scale
        qb, kb = q.astype(BF16), k.astype(BF16)
        qr_scr[rows, :] = qb
        kr_scr[rows, :] = kb
        v = v_ref[rows, :]
        s = _dot_nt(qb, kb) * decay
        inner = _dot(s.astype(BF16), v)
        cross = _dot(qb, state.astype(BF16)) * xi_f
        acc_scr[rows, :] = inner + cross
        kv = _dot((k * zeta_f).T.astype(BF16), v)
        return state * cdec_f + kv

    lax.fori_loop(0, n_chunks, fwd, jnp.zeros((RET_HEAD_DIM, RET_HEAD_DIM), F32), unroll=RET_UNROLL)

    def bwd(t, state):
        c = n_chunks - 1 - t
        rows = pl.ds(pl.multiple_of(c * C, C), C)
        qb = qr_scr[rows, :]
        v = v_ref[rows, :]
        y = acc_scr[rows, :] + _dot(qb, state.astype(BF16)) * xi_b
        yn = _rms(y, gain_ref[...])
        g = g_ref[rows, :].astype(F32)
        o_ref[rows, :] = (yn * (g * jax.nn.sigmoid(g))).astype(o_ref.dtype)
        kv = _dot((kr_scr[rows, :].astype(F32) * zeta_b).T.astype(BF16), v)
        return state * cdec_b + kv

    lax.fori_loop(0, n_chunks, bwd, jnp.zeros((RET_HEAD_DIM, RET_HEAD_DIM), F32), unroll=RET_UNROLL)


def _retention(proj, cos, sin, decay_logit, gain, layer):
    B, S, _ = proj.shape
    hd = RET_HEAD_DIM
    nh = RET_HEADS

    def col(part):
        return pl.BlockSpec((None, S, hd), lambda b, h: (b, 0, COL_RET // hd + part * nh + h))

    return pl.pallas_call(
        functools.partial(_ret_kernel, seq=S),
        grid=(B, nh),
        in_specs=[
            col(0), col(1), col(2), col(3),
            pl.BlockSpec((S, hd), lambda b, h: (0, 0)),
            pl.BlockSpec((S, hd), lambda b, h: (0, 0)),
            pl.BlockSpec((None, None, 2, hd), lambda b, h: (layer, h, 0, 0)),
            pl.BlockSpec((None, None, 1, hd), lambda b, h: (layer, h, 0, 0)),
        ],
        out_specs=pl.BlockSpec((None, S, hd), lambda b, h: (b, 0, h)),
        out_shape=jax.ShapeDtypeStruct((B, S, RET_WIDTH), BF16),
        scratch_shapes=[pltpu.VMEM((S, hd), F32), pltpu.VMEM((S, hd), BF16), pltpu.VMEM((S, hd), BF16)],
        compiler_params=_cparams(("parallel", "arbitrary")),
        name="retention",
    )(proj, proj, proj, proj, cos, sin, decay_logit, gain)


def _mla_up_kernel(cq_ref, ckv_ref, kr_ref, qn_ref, kvn_ref, wq_ref, wkv_ref, cos_ref, sina_ref, sinb_ref,
                   q_out, k_out, vt_out):
    scale = (MLA_NOPE + MLA_ROPE) ** -0.5 * math.log2(math.e)
    cosp, sina, sinb = cos_ref[...], sina_ref[...], sinb_ref[...]

    def rope(x):
        return x * cosp + pltpu.roll(x, MLA_ROPE // 2, 1) * sina + pltpu.roll(x, 128 - MLA_ROPE // 2, 1) * sinb

    q = _dot(_rms(cq_ref[...].astype(F32), qn_ref[...]).astype(BF16), wq_ref[...])
    kv = _dot(_rms(ckv_ref[...].astype(F32), kvn_ref[...]).astype(BF16), wkv_ref[...])
    k_rope = rope(kr_ref[...].astype(F32)).astype(BF16)
    for h in range(MLA_HEADS):
        lo = h * MLA_QK_PAD
        q_out[:, lo:lo + MLA_NOPE] = (q[:, lo:lo + MLA_NOPE] * scale).astype(BF16)
        q_out[:, lo + MLA_NOPE:lo + MLA_QK_PAD] = (rope(q[:, lo + MLA_NOPE:lo + MLA_QK_PAD]) * scale).astype(BF16)
        k_out[:, lo:lo + MLA_NOPE] = kv[:, h * MLA_NOPE:(h + 1) * MLA_NOPE].astype(BF16)
        k_out[:, lo + MLA_NOPE:lo + MLA_QK_PAD] = k_rope
    v_t = kv[:, MLA_HEADS * MLA_NOPE:].T
    ones = jnp.ones((MLA_VT_ROWS - MLA_V, v_t.shape[1]), BF16)
    for h in range(MLA_HEADS):
        lo = h * MLA_VT_ROWS
        vt_out[lo:lo + MLA_V, :] = v_t[h * MLA_V:(h + 1) * MLA_V, :].astype(BF16)
        vt_out[lo + MLA_V:lo + MLA_VT_ROWS, :] = ones


def _mla_up(proj2d, qn, kvn, wq, wkv, cosp, sina, sinb, layer, seq):
    T = proj2d.shape[0]
    tm = TM_UP
    n_s = seq // tm
    qk_w = MLA_HEADS * MLA_QK_PAD
    kv_w = MLA_HEADS * (MLA_NOPE + MLA_V)
    tab = pl.BlockSpec((tm, 128), lambda i: (i % n_s, 0))
    return pl.pallas_call(
        _mla_up_kernel,
        grid=(T // tm,),
        in_specs=[
            pl.BlockSpec((tm, MLA_Q_LORA), lambda i: (i, COL_CQ // MLA_Q_LORA)),
            pl.BlockSpec((tm, MLA_KV_LORA), lambda i: (i, COL_CKV // MLA_KV_LORA)),
            pl.BlockSpec((tm, 128), lambda i: (i, COL_KR // 128)),
            pl.BlockSpec((None, 1, MLA_Q_LORA), lambda i: (layer, 0, 0)),
            pl.BlockSpec((None, 1, MLA_KV_LORA), lambda i: (layer, 0, 0)),
            pl.BlockSpec((None, MLA_Q_LORA, qk_w), lambda i: (layer, 0, 0)),
            pl.BlockSpec((None, MLA_KV_LORA, kv_w), lambda i: (layer, 0, 0)),
            tab, tab, tab,
        ],
        out_specs=[
            pl.BlockSpec((tm, qk_w), lambda i: (i, 0)),
            pl.BlockSpec((tm, qk_w), lambda i: (i, 0)),
            pl.BlockSpec((None, MLA_HEADS * MLA_VT_ROWS, tm), lambda i: (i // n_s, 0, i % n_s)),
        ],
        out_shape=[
            jax.ShapeDtypeStruct((T, qk_w), BF16),
            jax.ShapeDtypeStruct((T, qk_w), BF16),
            jax.ShapeDtypeStruct((T // seq, MLA_HEADS * MLA_VT_ROWS, seq), BF16),
        ],
        compiler_params=_cparams(("parallel",)),
        name="mla_up",
    )(proj2d, proj2d, proj2d, qn, kvn, wq, wkv, cosp, sina, sinb)


def _attn_kernel(q_ref, k_ref, vt_ref, o_ref, acc_scr, st_a, st_b, *, seq):
    n = seq // TK
    U = ATTN_CHUNKS_PER_BODY
    q = q_ref[...]
    acc_scr[...] = jnp.zeros(acc_scr.shape, F32)
    bufs = (st_a, st_b)

    def scores(c, dst):
        r0 = pl.multiple_of(c * TK, TK)
        dst[...] = _dot_nt(k_ref[pl.ds(r0, TK), :], q)

    def absorb(c, src, m):
        r0 = pl.multiple_of(c * TK, TK)
        m_new = jnp.maximum(m, jnp.max(src[...], axis=0, keepdims=True))
        alpha = jnp.exp2(m - m_new)
        for qb in range(TQ // ATTN_STRIP):
            qs = slice(qb * ATTN_STRIP, (qb + 1) * ATTN_STRIP)
            a = alpha[:, qs] * acc_scr[:, qs]
            for kb in range(TK // ATTN_STRIP):
                ks = slice(kb * ATTN_STRIP, (kb + 1) * ATTN_STRIP)
                pt = jnp.exp2(src[ks, qs] - m_new[:, qs]).astype(BF16)
                a = a + _dot(vt_ref[:, pl.ds(r0 + kb * ATTN_STRIP, ATTN_STRIP)], pt)
            acc_scr[:, qs] = a
        return m_new

    scores(0, st_a)

    def body(jj, m):
        for u in range(U):
            scores(U * jj + u + 1, bufs[(u + 1) % 2])
            m = absorb(U * jj + u, bufs[u % 2], m)
        return m

    m = lax.fori_loop(0, n // U - 1, body, jnp.full((1, TQ), -jnp.inf, F32))
    for u in range(U):
        c = n - U + u
        if u + 1 < U:
            scores(c + 1, bufs[(u + 1) % 2])
        m = absorb(c, bufs[u % 2], m)
    acc = acc_scr[...]
    out_t = acc[0:MLA_V, :] / acc[MLA_V:MLA_V + 1, :]
    o_ref[...] = out_t.T.astype(o_ref.dtype)


def _attention(q, k, vt):
    B, S, _ = q.shape
    assert ATTN_CHUNKS_PER_BODY % 2 == 0 and (S // TK) % ATTN_CHUNKS_PER_BODY == 0
    return pl.pallas_call(
        functools.partial(_attn_kernel, seq=S),
        grid=(B, MLA_HEADS, S // TQ),
        in_specs=[
            pl.BlockSpec((None, TQ, MLA_QK_PAD), lambda b, h, i: (b, i, h)),
            pl.BlockSpec((None, S, MLA_QK_PAD), lambda b, h, i: (b, 0, h)),
            pl.BlockSpec((None, MLA_VT_ROWS, S), lambda b, h, i: (b, h, 0)),
        ],
        out_specs=pl.BlockSpec((None, TQ, MLA_V), lambda b, h, i: (b, i, h)),
        out_shape=jax.ShapeDtypeStruct((B, S, MLA_HEADS * MLA_V), BF16),
        scratch_shapes=[pltpu.VMEM((MLA_VT_ROWS, TQ), F32), pltpu.VMEM((TK, TQ), F32), pltpu.VMEM((TK, TQ), F32)],
        compiler_params=_cparams(("parallel", "parallel", "arbitrary")),
        name="attention",
    )(q, k, vt)


def _pool_kernel(u_ref, w_ref, sc_ref, o_ref, pad_scr, *, seq):
    gi = pl.program_id(1)
    half = jnp.int32(0)
    for k, w in enumerate(POOL_WINDOWS):
        half = jnp.where(gi == k, w // 2, half)
    R, H = POOL_ROWS, POOL_HALO
    pad_scr[0:H, :] = jnp.zeros((H, POOL_GROUP), BF16)
    pad_scr[H + seq:H + seq + H, :] = jnp.zeros((H, POOL_GROUP), BF16)
    pad_scr[H:H + seq, :] = u_ref[...]
    ii = lax.broadcasted_iota(jnp.int32, (R, R + 2 * H), 0)
    jj = lax.broadcasted_iota(jnp.int32, (R, R + 2 * H), 1)
    band = ((jj >= ii + H - half) & (jj < ii + H + half)).astype(BF16)
    rr = lax.broadcasted_iota(jnp.int32, (R, 1), 0)

    def body(t, carry):
        r0 = pl.multiple_of(t * R, R)
        total = _dot(band, pad_scr[pl.ds(r0, R + 2 * H), :])
        pos = r0 + rr
        cnt = (jnp.minimum(pos + half, seq) - jnp.maximum(pos - half, 0)).astype(F32)
        mixed = total / cnt - u_ref[pl.ds(r0, R), :].astype(F32)
        y = _dot(mixed.astype(BF16), w_ref[...]) * sc_ref[...]
        o_ref[pl.ds(r0, R), :] = y.astype(o_ref.dtype)
        return carry

    lax.fori_loop(0, seq // R, body, 0)


def _pool(proj, pool_w, pool_scale, layer):
    B, S, _ = proj.shape
    ng = len(POOL_WINDOWS)
    return pl.pallas_call(
        functools.partial(_pool_kernel, seq=S),
        grid=(B, ng),
        in_specs=[
            pl.BlockSpec((None, S, POOL_GROUP), lambda b, g: (b, 0, COL_POOL // POOL_GROUP + g)),
            pl.BlockSpec((None, None, POOL_GROUP, POOL_GROUP), lambda b, g: (layer, g, 0, 0)),
            pl.BlockSpec((None, 1, POOL_GROUP), lambda b, g: (layer, 0, g)),
        ],
        out_specs=pl.BlockSpec((None, S, POOL_GROUP), lambda b, g: (b, 0, g)),
        out_shape=jax.ShapeDtypeStruct((B, S, POOL_WIDTH), BF16),
        scratch_shapes=[pltpu.VMEM((S + 2 * POOL_HALO, POOL_GROUP), BF16)],
        compiler_params=_cparams(("parallel", "arbitrary")),
        name="pool",
    )(proj, pool_w, pool_scale)


def _merge_kernel(y0_ref, y1_ref, y2_ref, g0_ref, g1_ref, g2_ref, w0_ref, w1_ref, w2_ref, o_ref):
    acc = g0_ref[...].astype(F32) * _dot(y0_ref[...], w0_ref[...])
    acc += g1_ref[...].astype(F32) * _dot(y1_ref[...], w1_ref[...])
    acc += g2_ref[...].astype(F32) * _dot(y2_ref[...], w2_ref[...])
    o_ref[...] = acc.astype(o_ref.dtype)


def _merge(y_ret, y_mla, y_pool, gates, w_branch, layer):
    T, W = y_ret.shape
    D = D_MODEL
    tm, tn = TM_MERGE, TN_MERGE
    ysp = pl.BlockSpec((tm, W), lambda i, j: (i, 0))

    def gate(b):
        return pl.BlockSpec((tm, tn), lambda i, j: (i, b * D // tn + j))

    def wsp(b):
        return pl.BlockSpec((None, None, W, tn), lambda i, j: (layer, b, 0, j))

    return pl.pallas_call(
        _merge_kernel,
        grid=(T // tm, D // tn),
        in_specs=[ysp, ysp, ysp, gate(0), gate(1), gate(2), wsp(0), wsp(1), wsp(2)],
        out_specs=pl.BlockSpec((tm, tn), lambda i, j: (i, j)),
        out_shape=jax.ShapeDtypeStruct((T, D), BF16),
        compiler_params=_cparams(("parallel", "arbitrary")),
        name="merge",
    )(y_ret, y_mla, y_pool, gates, gates, gates, w_branch, w_branch, w_branch)


def _outproj_router_kernel(m_ref, w_ref, x_ref, ln_ref, w_hi_ref, w_lo_ref, b_ref, o_ref, cls_ref, wlo_ref, whi_ref):
    x = x_ref[...] + _dot(m_ref[...], w_ref[...])
    o_ref[...] = x
    h = _rms(x, ln_ref[...])
    h_hi = h.astype(BF16)
    h_lo = (h - h_hi.astype(F32)).astype(BF16)
    lt = _dot_nt(w_hi_ref[...], h_hi) + _dot_nt(w_hi_ref[...], h_lo) + _dot_nt(w_lo_ref[...], h_hi)
    lt = lt + b_ref[...]
    grp = [lt[g:g + 1, :] for g in range(N_GROUPS)]
    gmax = functools.reduce(jnp.maximum, grp)
    gden = functools.reduce(lambda a, b: a + b, [jnp.exp(x - gmax) for x in grp])
    g_p = 1.0 / gden
    gidx = jnp.full(gmax.shape, N_GROUPS, jnp.int32)
    for g in reversed(range(N_GROUPS)):
        gidx = jnp.where(grp[g] == gmax, g, gidx)
    sel = []
    for j in range(EXPERTS_PER_GROUP):
        e = jnp.zeros_like(gmax)
        for g in range(N_GROUPS):
            r = N_GROUPS + g * EXPERTS_PER_GROUP + j
            e = jnp.where(gidx == g, lt[r:r + 1, :], e)
        sel.append(e)
    emax = functools.reduce(jnp.maximum, sel)
    ex = [jnp.exp(e - emax) for e in sel]
    eden = functools.reduce(lambda a, b: a + b, ex)
    pe = [e / eden for e in ex]

    def top1(vals):
        best = functools.reduce(jnp.maximum, vals)
        idx = jnp.full(best.shape, len(vals), jnp.int32)
        for j in reversed(range(len(vals))):
            idx = jnp.where(vals[j] == best, j, idx)
        return best, idx

    p1, i1 = top1(pe)
    p2, i2 = top1([jnp.where(i1 == j, -1.0, pe[j]) for j in range(EXPERTS_PER_GROUP)])
    den = p1 + p2
    w1 = g_p * (p1 / den)
    w2 = g_p * (p2 / den)
    lo = jnp.minimum(i1, i2)
    hi = jnp.maximum(i1, i2)
    pair = jnp.zeros_like(lo)
    for n, (a, b) in enumerate(PAIRS):
        pair = jnp.where((lo == a) & (hi == b), n, pair)
    cls_ref[...] = gidx * len(PAIRS) + pair
    first_is_lo = i1 < i2
    wlo_ref[...] = jnp.where(first_is_lo, w1, w2)
    whi_ref[...] = jnp.where(first_is_lo, w2, w1)


def _outproj_router(merged, w_out, x, ln, w_hi, w_lo, bias, layer):
    T, D = x.shape
    tm = TM_OUT
    n_log = N_GROUPS + N_EXPERTS
    rows = pl.BlockSpec((tm, D), lambda i: (i, 0))
    row = pl.BlockSpec((None, 1, tm), lambda i: (i, 0, 0))
    outs = pl.pallas_call(
        _outproj_router_kernel,
        grid=(T // tm,),
        in_specs=[
            rows,
            pl.BlockSpec((None, D, D), lambda i: (layer, 0, 0)),
            rows,
            pl.BlockSpec((None, 1, D), lambda i: (layer, 0, 0)),
            pl.BlockSpec((None, n_log, D), lambda i: (layer, 0, 0)),
            pl.BlockSpec((None, n_log, D), lambda i: (layer, 0, 0)),
            pl.BlockSpec((None, n_log, 1), lambda i: (layer, 0, 0)),
        ],
        out_specs=[rows, row, row, row],
        out_shape=[
            jax.ShapeDtypeStruct((T, D), F32),
            jax.ShapeDtypeStruct((T // tm, 1, tm), jnp.int32),
            jax.ShapeDtypeStruct((T // tm, 1, tm), F32),
            jax.ShapeDtypeStruct((T // tm, 1, tm), F32),
        ],
        compiler_params=_cparams(("parallel",)),
        name="outproj_router",
    )(merged, w_out, x, ln, w_hi, w_lo, bias)
    return outs[0], outs[1].reshape(T), outs[2].reshape(T), outs[3].reshape(T)


def _moe_kernel(elo_ref, ehi_ref, nused_ref, tok_ref, dst_ref,
                x_hbm, ws_ref, ln_ref, wg_lo, wu_lo, wd_lo, wg_hi, wu_hi, wd_hi,
                o_hbm, xbuf, obuf, gsem, ssem):
    del elo_ref, ehi_ref
    bm = MOE_BLOCK
    i = pl.program_id(0)
    n_used = nused_ref[0]
    cur = i % 2

    def start_gather(block, buf):
        for r in range(bm):
            pltpu.make_async_copy(x_hbm.at[tok_ref[block * bm + r]], xbuf.at[buf, r], gsem.at[buf]).start()

    def wait_gather(buf):
        pltpu.make_async_copy(x_hbm.at[pl.ds(0, bm)], xbuf.at[buf], gsem.at[buf]).wait()

    def start_scatter(block, buf):
        for r in range(bm):
            pltpu.make_async_copy(obuf.at[buf, r], o_hbm.at[dst_ref[(block + 1) * bm + r]], ssem).start()

    def wait_scatter():
        pltpu.make_async_copy(obuf.at[0], o_hbm.at[pl.ds(0, bm)], ssem).wait()

    @pl.when(i == 0)
    def _():
        obuf[...] = jnp.zeros(obuf.shape, F32)
        start_gather(0, 0)
        start_scatter(-1, 1)

    @pl.when(i < n_used)
    def _():
        wait_gather(cur)
        wait_scatter()
        start_gather(jnp.minimum(i + 1, n_used - 1), 1 - cur)
        start_scatter(i - 1, 1 - cur)
        x = xbuf[cur]
        h = _rms(x, ln_ref[...]).astype(BF16)

        def ffn(wg, wu, wd):
            a = _dot(h, wg[...])
            hidden = (a * jax.nn.sigmoid(a)) * _dot(h, wu[...])
            return _dot(hidden.astype(BF16), wd[...])

        ws = ws_ref[...]
        y = ws[:, 0:1] * ffn(wg_lo, wu_lo, wd_lo) + ws[:, 1:2] * ffn(wg_hi, wu_hi, wd_hi)
        obuf[cur] = x + y

        @pl.when(i == n_used - 1)
        def _():
            wait_scatter()
            start_scatter(i, cur)
            wait_scatter()
            wait_gather(1 - cur)


def _moe(x, elo, ehi, n_used, slot_tok, slot_dst, ws, ln, w_gate, w_up, w_down, layer):
    T, D = x.shape
    bm = MOE_BLOCK
    n_blocks = elo.shape[0]
    Hd = EXPERT_HIDDEN

    def wspec(shape, which):
        def imap(i, elo_r, ehi_r, *_):
            e = elo_r[i] if which == 0 else ehi_r[i]
            return (layer, e, 0, 0)
        return pl.BlockSpec((None, None) + shape, imap)

    col = pl.BlockSpec((bm, 2), lambda i, *_: (i, 0))
    grid_spec = pltpu.PrefetchScalarGridSpec(
        num_scalar_prefetch=5,
        grid=(n_blocks,),
        in_specs=[
            pl.BlockSpec(memory_space=pl.ANY),
            col,
            pl.BlockSpec((None, 1, D), lambda i, *_: (layer, 0, 0)),
            wspec((D, Hd), 0), wspec((D, Hd), 0), wspec((Hd, D), 0),
            wspec((D, Hd), 1), wspec((D, Hd), 1), wspec((Hd, D), 1),
        ],
        out_specs=pl.BlockSpec(memory_space=pl.ANY),
        scratch_shapes=[
            pltpu.VMEM((2, bm, D), F32), pltpu.VMEM((2, bm, D), F32),
            pltpu.SemaphoreType.DMA((2,)), pltpu.SemaphoreType.DMA,
        ],
    )
    return pl.pallas_call(
        _moe_kernel,
        grid_spec=grid_spec,
        out_shape=jax.ShapeDtypeStruct((T + bm, D), F32),
        compiler_params=_cparams(("arbitrary",)),
        name="moe",
    )(elo, ehi, n_used, slot_tok, slot_dst, x, ws, ln, w_gate, w_up, w_down, w_gate, w_up, w_down)


def _dispatch_plan(cls, wlo, whi):
    T = cls.shape[0]
    bm = MOE_BLOCK
    n_blocks = (T + N_CLASSES * (bm - 1)) // bm
    n_slots = n_blocks * bm
    _, order, wlo_s, whi_s = lax.sort((cls, jnp.arange(T, dtype=jnp.int32), wlo, whi), num_keys=1, is_stable=True)
    packed = jnp.stack([order, lax.bitcast_convert_type(wlo_s, jnp.int32),
                        lax.bitcast_convert_type(whi_s, jnp.int32)], axis=-1)
    counts = jnp.sum(cls[None, :] == jnp.arange(N_CLASSES, dtype=jnp.int32)[:, None], axis=1, dtype=jnp.int32)
    padded = (counts + bm - 1) // bm * bm
    pends = jnp.cumsum(padded)
    pstarts = pends - padded
    starts = jnp.cumsum(counts) - counts
    n_used = pends[-1] // bm
    blk = jnp.arange(n_blocks, dtype=jnp.int32)
    bidx = jnp.minimum(blk, n_used - 1)
    bcls = jnp.minimum(jnp.sum(pends[None, :] <= (bidx * bm)[:, None], axis=1, dtype=jnp.int32), N_CLASSES - 1)
    onehot = bcls[:, None] == jnp.arange(N_CLASSES, dtype=jnp.int32)[None, :]

    def per_block(table):
        return jnp.sum(jnp.where(onehot, table[None, :], 0), axis=1, dtype=jnp.int32)

    off = bidx * bm - per_block(pstarts)
    nval = jnp.where(blk < n_used, jnp.clip(per_block(counts) - off, 0, bm), 0)
    r = jnp.arange(bm, dtype=jnp.int32)[None, :]
    valid = r < nval[:, None]
    src = jnp.clip((per_block(starts) + off)[:, None] + r, 0, T - 1)
    got = packed[src]
    tok2d = jnp.where(valid, got[..., 0], 0)
    dst2d = jnp.where(valid, tok2d, T + r)
    slot_tok = tok2d.reshape(n_slots)
    slot_dst = jnp.concatenate([T + r[0], dst2d.reshape(n_slots)])
    grp = bcls // len(PAIRS)
    pair = bcls % len(PAIRS)
    pair_lo, pair_hi = jnp.zeros_like(pair), jnp.zeros_like(pair)
    for n, (a, b) in enumerate(PAIRS):
        pair_lo = jnp.where(pair == n, a, pair_lo)
        pair_hi = jnp.where(pair == n, b, pair_hi)
    elo = grp * EXPERTS_PER_GROUP + pair_lo
    ehi = grp * EXPERTS_PER_GROUP + pair_hi
    ws = lax.bitcast_convert_type(got[..., 1:3], F32).reshape(n_slots, 2)
    return elo, ehi, n_used.reshape(1).astype(jnp.int32), slot_tok, slot_dst, ws


def _ple_kernel(x_ref, p_ref, ln_ref, wg_ref, wp_ref, fin_ref, o_ref, *, final_norm):
    x = x_ref[...]
    gate = jax.nn.sigmoid(_dot(_rms(x, ln_ref[...]).astype(BF16), wg_ref[...]))
    y = x + gate * _dot(p_ref[...].astype(BF16), wp_ref[...])
    o_ref[...] = _rms(y, fin_ref[...]) if final_norm else y


def _ple(x, p, ln, w_gate, w_proj, fin_gain, layer, final_norm):
    T, D = p.shape[1], x.shape[1]
    return pl.pallas_call(
        functools.partial(_ple_kernel, final_norm=final_norm),
        grid=(T // TM_PLE,),
        in_specs=[
            pl.BlockSpec((TM_PLE, D), lambda i: (i, 0)),
            pl.BlockSpec((None, TM_PLE, PLE_DIM), lambda i: (layer, i, 0)),
            pl.BlockSpec((None, 1, D), lambda i: (layer, 0, 0)),
            pl.BlockSpec((None, D, D), lambda i: (layer, 0, 0)),
            pl.BlockSpec((None, PLE_DIM, D), lambda i: (layer, 0, 0)),
            pl.BlockSpec((1, D), lambda i: (0, 0)),
        ],
        out_specs=pl.BlockSpec((TM_PLE, D), lambda i: (i, 0)),
        out_shape=jax.ShapeDtypeStruct((T, D), F32),
        compiler_params=_cparams(("parallel",)),
        name="ple_final" if final_norm else "ple",
    )(x, p, ln, w_gate, w_proj, fin_gain)


def _rope_tables(seq, dim):
    inv = 1.0 / (ROPE_BASE ** (jnp.arange(0, dim, 2, dtype=F32) / dim))
    ang = jnp.arange(seq, dtype=F32)[:, None] * inv[None, :]
    return jnp.cos(ang), jnp.sin(ang)


def _prep_in_weights(w_in, b_gate):
    depth, D, _ = w_in.shape
    ret_w = 4 * RET_WIDTH
    mla_w = MLA_Q_LORA + MLA_KV_LORA + MLA_ROPE
    pad = COL_POOL - COL_KR - MLA_ROPE
    w = jnp.concatenate([
        w_in[..., :ret_w + mla_w],
        jnp.zeros((depth, D, pad), w_in.dtype),
        w_in[..., ret_w + mla_w:],
    ], axis=-1).astype(BF16)
    b = jnp.concatenate([jnp.zeros((depth, COL_GATE), F32), b_gate], axis=-1).reshape(depth, 1, IN_PAD_WIDTH)
    return w, b


def _prep_mla_weights(w_uq, w_ukv):
    depth = w_uq.shape[0]
    wq = w_uq.reshape(depth, MLA_Q_LORA, MLA_HEADS, MLA_NOPE + MLA_ROPE)
    wq = jnp.pad(wq, ((0, 0), (0, 0), (0, 0), (0, MLA_QK_PAD - MLA_NOPE - MLA_ROPE)))
    wq = wq.reshape(depth, MLA_Q_LORA, MLA_HEADS * MLA_QK_PAD).astype(BF16)
    wkv = w_ukv.reshape(depth, MLA_KV_LORA, MLA_HEADS, MLA_NOPE + MLA_V)
    wkv = jnp.concatenate([
        wkv[..., :MLA_NOPE].reshape(depth, MLA_KV_LORA, MLA_HEADS * MLA_NOPE),
        wkv[..., MLA_NOPE:].reshape(depth, MLA_KV_LORA, MLA_HEADS * MLA_V),
    ], axis=-1).astype(BF16)
    return wq, wkv


def _split_bf16(w):
    hi = w.astype(BF16)
    lo = (w - hi.astype(F32)).astype(BF16)
    return hi, lo


def kernel(x, p, ln_mix, w_in, b_gate, ret_decay_logit, ret_norm, mla_q_norm, mla_w_uq, mla_kv_norm, mla_w_ukv, pool_w, pool_scale, w_branch, w_out, ln_moe, w_grp, b_grp, w_rt, b_rt, w_exp_gate, w_exp_up, w_exp_down, ln_ple, w_ple_gate, w_ple_proj, ln_final):
    B, S, D = x.shape
    depth = p.shape[0]
    T = B * S
    assert D == D_MODEL and S % max(TQ, TK, POOL_ROWS, TM_UP) == 0
    assert all(T % t == 0 for t in (TM_IN, TM_MERGE, TM_OUT, TM_PLE, MOE_BLOCK))

    cos_r, sin_r = _rope_tables(S, RET_HEAD_DIM)
    ret_cos = jnp.concatenate([cos_r, cos_r], axis=-1)
    ret_sin = jnp.concatenate([-sin_r, sin_r], axis=-1)
    cos_m, sin_m = _rope_tables(S, MLA_ROPE)
    z32 = jnp.zeros_like(sin_m)
    z64 = jnp.zeros((S, 128 - MLA_ROPE), F32)
    mla_cos = jnp.concatenate([cos_m, cos_m, z64], axis=-1)
    mla_sina = jnp.concatenate([z32, sin_m, z64], axis=-1)
    mla_sinb = jnp.concatenate([-sin_m, z32, z64], axis=-1)

    w_in_p, b_in = _prep_in_weights(w_in, b_gate)
    wq, wkv = _prep_mla_weights(mla_w_uq, mla_w_ukv)
    ln_mix3 = ln_mix.reshape(depth, 1, D)
    ln_moe3 = ln_moe.reshape(depth, 1, D)
    ln_ple3 = ln_ple.reshape(depth, 1, D)
    qn3 = mla_q_norm.reshape(depth, 1, MLA_Q_LORA)
    kvn3 = mla_kv_norm.reshape(depth, 1, MLA_KV_LORA)
    decay4 = jnp.broadcast_to(ret_decay_logit.transpose(0, 2, 1)[..., None], (depth, RET_HEADS, 2, RET_HEAD_DIM))
    gain4 = ret_norm.reshape(depth, RET_HEADS, 1, RET_HEAD_DIM)
    pool_w_b = pool_w.astype(BF16)
    pool_sc3 = pool_scale.reshape(depth, 1, POOL_WIDTH)
    w_branch_b = w_branch.astype(BF16)
    w_out_b = w_out.astype(BF16)
    w_router = jnp.concatenate([w_grp, w_rt], axis=-1).transpose(0, 2, 1)
    w_router_hi, w_router_lo = _split_bf16(w_router)
    b_router = jnp.concatenate([b_grp, b_rt], axis=-1).reshape(depth, N_GROUPS + N_EXPERTS, 1)
    w_eg = w_exp_gate.astype(BF16)
    w_eu = w_exp_up.astype(BF16)
    w_ed = w_exp_down.astype(BF16)
    w_pg = w_ple_gate.astype(BF16)
    w_pp = w_ple_proj.astype(BF16)
    p3 = p.reshape(depth, T, PLE_DIM)

    xt = x.reshape(T, D)
    for layer in range(depth):
        proj = _inproj(xt, ln_mix3, w_in_p, b_in, layer, 0, COL_GATE, False)
        gates = _inproj(xt, ln_mix3, w_in_p, b_in, layer, COL_GATE, N_BRANCH * D, True)
        proj3 = proj.reshape(B, S, COL_GATE)
        y_ret = _retention(proj3, ret_cos, ret_sin, decay4, gain4, layer)
        q, k, vt = _mla_up(proj, qn3, kvn3, wq, wkv, mla_cos, mla_sina, mla_sinb, layer, S)
        y_mla = _attention(q.reshape(B, S, -1), k.reshape(B, S, -1), vt)
        y_pool = _pool(proj3, pool_w_b, pool_sc3, layer)
        merged = _merge(y_ret.reshape(T, -1), y_mla.reshape(T, -1), y_pool.reshape(T, -1), gates, w_branch_b, layer)
        xt, cls, wlo, whi = _outproj_router(merged, w_out_b, xt, ln_moe3, w_router_hi, w_router_lo, b_router, layer)
        elo, ehi, n_used, slot_tok, slot_dst, ws = _dispatch_plan(cls, wlo, whi)
        xt = _moe(xt, elo, ehi, n_used, slot_tok, slot_dst, ws, ln_moe3, w_eg, w_eu, w_ed, layer)
        xt = _ple(xt, p3, ln_ple3, w_pg, w_pp, ln_final.reshape(1, D), layer, layer == depth - 1)
    return xt.reshape(B, S, D)
```

```python
import functools
import math

import jax
import jax.numpy as jnp
from jax import lax
from jax.experimental import pallas as pl
from jax.experimental.pallas import tpu as pltpu

F32 = jnp.float32
BF16 = jnp.bfloat16

D_MODEL = 2048
RET_HEADS = 8
RET_HEAD_DIM = 128
RET_WIDTH = RET_HEADS * RET_HEAD_DIM
RET_CHUNK = 128
MLA_HEADS = 8
MLA_Q_LORA = 512
MLA_KV_LORA = 256
MLA_NOPE = 128
MLA_ROPE = 64
MLA_V = 128
MLA_QK_PAD = 256
POOL_WINDOWS = (2, 4, 8, 16)
POOL_GROUP = 256
POOL_WIDTH = len(POOL_WINDOWS) * POOL_GROUP
N_BRANCH = 3
N_GROUPS = 8
EXPERTS_PER_GROUP = 4
N_EXPERTS = N_GROUPS * EXPERTS_PER_GROUP
EXPERT_HIDDEN = 512
PAIRS = ((0, 1), (0, 2), (0, 3), (1, 2), (1, 3), (2, 3))
N_CLASSES = N_GROUPS * len(PAIRS)
PLE_DIM = 256
ROPE_BASE = 10000.0
NORM_EPS = 1e-6

COL_RET = 0
COL_CQ = 4096
COL_CKV = COL_CQ + MLA_Q_LORA
COL_KR = COL_CKV + MLA_KV_LORA
COL_POOL = 5120
COL_GATE = 6144
IN_PAD_WIDTH = COL_GATE + N_BRANCH * D_MODEL

V7X_VMEM_LIMIT = 56 * 1024 * 1024

TM_IN, TN_IN = 1024, 2048
TM_MERGE, TN_MERGE = 512, 2048
TM_OUT = 512
TQ = 2048
TK = 512
ATTN_CHUNKS_PER_BODY = 2
ATTN_STRIP = 256
TM_UP = 512
MLA_VT_ROWS = MLA_V + 16
RET_UNROLL = 16
POOL_ROWS = 512
POOL_HALO = 128
MOE_BLOCK = 256
ROW_EXTRA = 128
TM_PLE = 512


def _cparams(sem):
    return pltpu.CompilerParams(dimension_semantics=sem, vmem_limit_bytes=V7X_VMEM_LIMIT)


def _rms(x, g):
    return x * lax.rsqrt(jnp.mean(x * x, axis=-1, keepdims=True) + NORM_EPS) * g


def _dot(a, b):
    return jnp.dot(a, b, preferred_element_type=F32)


def _dot_nt(a, b):
    return lax.dot_general(a, b, (((1,), (1,)), ((), ())), preferred_element_type=F32)


def _inproj_kernel(x_ref, g_ref, w_ref, b_ref, o_ref, h_scr, *, gated):
    @pl.when(pl.program_id(1) == 0)
    def _():
        h_scr[...] = _rms(x_ref[...], g_ref[...]).astype(BF16)

    acc = _dot(h_scr[...], w_ref[...])
    if gated:
        acc = jax.nn.sigmoid(acc + b_ref[...])
    o_ref[...] = acc.astype(o_ref.dtype)


def _inproj(x, ln, w, b, layer, col0, ncols, gated):
    T, D = x.shape
    tm, tn = TM_IN, TN_IN
    assert col0 % tn == 0 and ncols % tn == 0
    j0 = col0 // tn
    return pl.pallas_call(
        functools.partial(_inproj_kernel, gated=gated),
        grid=(T // tm, ncols // tn),
        in_specs=[
            pl.BlockSpec((tm, D), lambda i, j: (i, 0)),
            pl.BlockSpec((None, 1, D), lambda i, j: (layer, 0, 0)),
            pl.BlockSpec((None, D, tn), lambda i, j: (layer, 0, j0 + j)),
            pl.BlockSpec((None, 1, tn), lambda i, j: (layer, 0, j0 + j)),
        ],
        out_specs=pl.BlockSpec((tm, tn), lambda i, j: (i, j)),
        out_shape=jax.ShapeDtypeStruct((T, ncols), BF16),
        scratch_shapes=[pltpu.VMEM((tm, D), BF16)],
        compiler_params=_cparams(("parallel", "arbitrary")),
        name="inproj_gate" if gated else "inproj",
    )(x, ln, w, b)


def _ret_kernel(q_ref, k_ref, v_ref, g_ref, cos_ref, sin_ref, lg_ref, gain_ref, o_ref,
                acc_scr, qr_scr, kr_scr, *, seq):
    C = RET_CHUNK
    n_chunks = seq // C
    lg = jax.nn.log_sigmoid(lg_ref[...])
    lgf, lgb = lg[0:1, :], lg[1:2, :]
    ii = lax.broadcasted_iota(jnp.int32, (C, C), 0).astype(F32)
    ll = lax.broadcasted_iota(jnp.int32, (C, C), 1).astype(F32)
    diff = ii - ll
    decay = jnp.where(diff >= 0, jnp.exp(lgf * jnp.maximum(diff, 0.0)),
                      jnp.exp(lgb * jnp.maximum(-diff, 0.0)))
    xi_f = jnp.exp(lgf * (ii + 1.0))
    xi_b = jnp.exp(lgb * (C - ii))
    zeta_f = jnp.exp(lgf * (C - 1.0 - ii))
    zeta_b = jnp.exp(lgb * ii)
    cdec_f = jnp.exp(lgf * C)
    cdec_b = jnp.exp(lgb * C)
    k_scale = RET_HEAD_DIM ** -0.5

    def rope(x, rows):
        return x * cos_ref[rows, :] + pltpu.roll(x, RET_HEAD_DIM // 2, 1) * sin_ref[rows, :]

    def fwd(c, state):
        rows = pl.ds(pl.multiple_of(c * C, C), C)
        q = rope(q_ref[rows, :].astype(F32), rows)
        k = rope(k_ref[rows, :].astype(F32), rows) * k_scale
        qb, kb = q.astype(BF16), k.astype(BF16)
        qr_scr[rows, :] = qb
        kr_scr[rows, :] = kb
        v = v_ref[rows, :]
        s = _dot_nt(qb, kb) * decay
        inner = _dot(s.astype(BF16), v)
        cross = _dot(qb, state.astype(BF16)) * xi_f
        acc_scr[rows, :] = inner + cross
        kv = _dot((k * zeta_f).T.astype(BF16), v)
        return state * cdec_f + kv

    lax.fori_loop(0, n_chunks, fwd, jnp.zeros((RET_HEAD_DIM, RET_HEAD_DIM), F32), unroll=RET_UNROLL)

    def bwd(t, state):
        c = n_chunks - 1 - t
        rows = pl.ds(pl.multiple_of(c * C, C), C)
        qb = qr_scr[rows, :]
        v = v_ref[rows, :]
        y = acc_scr[rows, :] + _dot(qb, state.astype(BF16)) * xi_b
        yn = _rms(y, gain_ref[...])
        g = g_ref[rows, :].astype(F32)
        o_ref[rows, :] = (yn * (g * jax.nn.sigmoid(g))).astype(o_ref.dtype)
        kv = _dot((kr_scr[rows, :].astype(F32) * zeta_b).T.astype(BF16), v)
        return state * cdec_b + kv

    lax.fori_loop(0, n_chunks, bwd, jnp.zeros((RET_HEAD_DIM, RET_HEAD_DIM), F32), unroll=RET_UNROLL)


def _retention(proj, cos, sin, decay_logit, gain, layer):
    B, S, _ = proj.shape
    hd = RET_HEAD_DIM
    nh = RET_HEADS

    def col(part):
        return pl.BlockSpec((None, S, hd), lambda b, h: (b, 0, COL_RET // hd + part * nh + h))

    return pl.pallas_call(
        functools.partial(_ret_kernel, seq=S),
        grid=(B, nh),
        in_specs=[
            col(0), col(1), col(2), col(3),
            pl.BlockSpec((S, hd), lambda b, h: (0, 0)),
            pl.BlockSpec((S, hd), lambda b, h: (0, 0)),
            pl.BlockSpec((None, None, 2, hd), lambda b, h: (layer, h, 0, 0)),
            pl.BlockSpec((None, None, 1, hd), lambda b, h: (layer, h, 0, 0)),
        ],
        out_specs=pl.BlockSpec((None, S, hd), lambda b, h: (b, 0, h)),
        out_shape=jax.ShapeDtypeStruct((B, S, RET_WIDTH), BF16),
        scratch_shapes=[pltpu.VMEM((S, hd), F32), pltpu.VMEM((S, hd), BF16), pltpu.VMEM((S, hd), BF16)],
        compiler_params=_cparams(("parallel", "arbitrary")),
        name="retention",
    )(proj, proj, proj, proj, cos, sin, decay_logit, gain)


def _mla_up_kernel(cq_ref, ckv_ref, kr_ref, qn_ref, kvn_ref, wq_ref, wkv_ref, cos_ref, sina_ref, sinb_ref,
                   q_out, k_out, vt_out):
    scale = (MLA_NOPE + MLA_ROPE) ** -0.5 * math.log2(math.e)
    cosp, sina, sinb = cos_ref[...], sina_ref[...], sinb_ref[...]

    def rope(x):
        return x * cosp + pltpu.roll(x, MLA_ROPE // 2, 1) * sina + pltpu.roll(x, 128 - MLA_ROPE // 2, 1) * sinb

    q = _dot(_rms(cq_ref[...].astype(F32), qn_ref[...]).astype(BF16), wq_ref[...])
    kv = _dot(_rms(ckv_ref[...].astype(F32), kvn_ref[...]).astype(BF16), wkv_ref[...])
    k_rope = rope(kr_ref[...].astype(F32)).astype(BF16)
    for h in range(MLA_HEADS):
        lo = h * MLA_QK_PAD
        q_out[:, lo:lo + MLA_NOPE] = (q[:, lo:lo + MLA_NOPE] * scale).astype(BF16)
        q_out[:, lo + MLA_NOPE:lo + MLA_QK_PAD] = (rope(q[:, lo + MLA_NOPE:lo + MLA_QK_PAD]) * scale).astype(BF16)
        k_out[:, lo:lo + MLA_NOPE] = kv[:, h * MLA_NOPE:(h + 1) * MLA_NOPE].astype(BF16)
        k_out[:, lo + MLA_NOPE:lo + MLA_QK_PAD] = k_rope
    v_t = kv[:, MLA_HEADS * MLA_NOPE:].T
    ones = jnp.ones((MLA_VT_ROWS - MLA_V, v_t.shape[1]), BF16)
    for h in range(MLA_HEADS):
        lo = h * MLA_VT_ROWS
        vt_out[lo:lo + MLA_V, :] = v_t[h * MLA_V:(h + 1) * MLA_V, :].astype(BF16)
        vt_out[lo + MLA_V:lo + MLA_VT_ROWS, :] = ones


def _mla_up(proj2d, qn, kvn, wq, wkv, cosp, sina, sinb, layer, seq):
    T = proj2d.shape[0]
    tm = TM_UP
    n_s = seq // tm
    qk_w = MLA_HEADS * MLA_QK_PAD
    kv_w = MLA_HEADS * (MLA_NOPE + MLA_V)
    tab = pl.BlockSpec((tm, 128), lambda i: (i % n_s, 0))
    return pl.pallas_call(
        _mla_up_kernel,
        grid=(T // tm,),
        in_specs=[
            pl.BlockSpec((tm, MLA_Q_LORA), lambda i: (i, COL_CQ // MLA_Q_LORA)),
            pl.BlockSpec((tm, MLA_KV_LORA), lambda i: (i, COL_CKV // MLA_KV_LORA)),
            pl.BlockSpec((tm, 128), lambda i: (i, COL_KR // 128)),
            pl.BlockSpec((None, 1, MLA_Q_LORA), lambda i: (layer, 0, 0)),
            pl.BlockSpec((None, 1, MLA_KV_LORA), lambda i: (layer, 0, 0)),
            pl.BlockSpec((None, MLA_Q_LORA, qk_w), lambda i: (layer, 0, 0)),
            pl.BlockSpec((None, MLA_KV_LORA, kv_w), lambda i: (layer, 0, 0)),
            tab, tab, tab,
        ],
        out_specs=[
            pl.BlockSpec((tm, qk_w), lambda i: (i, 0)),
            pl.BlockSpec((tm, qk_w), lambda i: (i, 0)),
            pl.BlockSpec((None, MLA_HEADS * MLA_VT_ROWS, tm), lambda i: (i // n_s, 0, i % n_s)),
        ],
        out_shape=[
            jax.ShapeDtypeStruct((T, qk_w), BF16),
            jax.ShapeDtypeStruct((T, qk_w), BF16),
            jax.ShapeDtypeStruct((T // seq, MLA_HEADS * MLA_VT_ROWS, seq), BF16),
        ],
        compiler_params=_cparams(("parallel",)),
        name="mla_up",
    )(proj2d, proj2d, proj2d, qn, kvn, wq, wkv, cosp, sina, sinb)


def _attn_kernel(q_ref, k_ref, vt_ref, o_ref, acc_scr, st_a, st_b, *, seq):
    n = seq // TK
    U = ATTN_CHUNKS_PER_BODY
    q = q_ref[...]
    acc_scr[...] = jnp.zeros(acc_scr.shape, F32)
    bufs = (st_a, st_b)

    def scores(c, dst):
        r0 = pl.multiple_of(c * TK, TK)
        dst[...] = _dot_nt(k_ref[pl.ds(r0, TK), :], q)

    def absorb(c, src, m):
        r0 = pl.multiple_of(c * TK, TK)
        m_new = jnp.maximum(m, jnp.max(src[...], axis=0, keepdims=True))
        alpha = jnp.exp2(m - m_new)
        for qb in range(TQ // ATTN_STRIP):
            qs = slice(qb * ATTN_STRIP, (qb + 1) * ATTN_STRIP)
            a = alpha[:, qs] * acc_scr[:, qs]
            for kb in range(TK // ATTN_STRIP):
                ks = slice(kb * ATTN_STRIP, (kb + 1) * ATTN_STRIP)
                pt = jnp.exp2(src[ks, qs] - m_new[:, qs]).astype(BF16)
                a = a + _dot(vt_ref[:, pl.ds(r0 + kb * ATTN_STRIP, ATTN_STRIP)], pt)
            acc_scr[:, qs] = a
        return m_new

    scores(0, st_a)

    def body(jj, m):
        for u in range(U):
            scores(U * jj + u + 1, bufs[(u + 1) % 2])
            m = absorb(U * jj + u, bufs[u % 2], m)
        return m

    m = lax.fori_loop(0, n // U - 1, body, jnp.full((1, TQ), -jnp.inf, F32))
    for u in range(U):
        c = n - U + u
        if u + 1 < U:
            scores(c + 1, bufs[(u + 1) % 2])
        m = absorb(c, bufs[u % 2], m)
    acc = acc_scr[...]
    out_t = acc[0:MLA_V, :] / acc[MLA_V:MLA_V + 1, :]
    o_ref[...] = out_t.T.astype(o_ref.dtype)


def _attention(q, k, vt):
    B, S, _ = q.shape
    assert ATTN_CHUNKS_PER_BODY % 2 == 0 and (S // TK) % ATTN_CHUNKS_PER_BODY == 0
    return pl.pallas_call(
        functools.partial(_attn_kernel, seq=S),
        grid=(B, MLA_HEADS, S // TQ),
        in_specs=[
            pl.BlockSpec((None, TQ, MLA_QK_PAD), lambda b, h, i: (b, i, h)),
            pl.BlockSpec((None, S, MLA_QK_PAD), lambda b, h, i: (b, 0, h)),
            pl.BlockSpec((None, MLA_VT_ROWS, S), lambda b, h, i: (b, h, 0)),
        ],
        out_specs=pl.BlockSpec((None, TQ, MLA_V), lambda b, h, i: (b, i, h)),
        out_shape=jax.ShapeDtypeStruct((B, S, MLA_HEADS * MLA_V), BF16),
        scratch_shapes=[pltpu.VMEM((MLA_VT_ROWS, TQ), F32), pltpu.VMEM((TK, TQ), F32), pltpu.VMEM((TK, TQ), F32)],
        compiler_params=_cparams(("parallel", "parallel", "arbitrary")),
        name="attention",
    )(q, k, vt)


def _pool_kernel(u_ref, w_ref, sc_ref, o_ref, pad_scr, *, seq):
    gi = pl.program_id(1)
    half = jnp.int32(0)
    for k, w in enumerate(POOL_WINDOWS):
        half = jnp.where(gi == k, w // 2, half)
    R, H = POOL_ROWS, POOL_HALO
    pad_scr[0:H, :] = jnp.zeros((H, POOL_GROUP), BF16)
    pad_scr[H + seq:H + seq + H, :] = jnp.zeros((H, POOL_GROUP), BF16)
    pad_scr[H:H + seq, :] = u_ref[...]
    ii = lax.broadcasted_iota(jnp.int32, (R, R + 2 * H), 0)
    jj = lax.broadcasted_iota(jnp.int32, (R, R + 2 * H), 1)
    band = ((jj >= ii + H - half) & (jj < ii + H + half)).astype(BF16)
    rr = lax.broadcasted_iota(jnp.int32, (R, 1), 0)

    def body(t, carry):
        r0 = pl.multiple_of(t * R, R)
        total = _dot(band, pad_scr[pl.ds(r0, R + 2 * H), :])
        pos = r0 + rr
        cnt = (jnp.minimum(pos + half, seq) - jnp.maximum(pos - half, 0)).astype(F32)
        mixed = total / cnt - u_ref[pl.ds(r0, R), :].astype(F32)
        y = _dot(mixed.astype(BF16), w_ref[...]) * sc_ref[...]
        o_ref[pl.ds(r0, R), :] = y.astype(o_ref.dtype)
        return carry

    lax.fori_loop(0, seq // R, body, 0)


def _pool(proj, pool_w, pool_scale, layer):
    B, S, _ = proj.shape
    ng = len(POOL_WINDOWS)
    return pl.pallas_call(
        functools.partial(_pool_kernel, seq=S),
        grid=(B, ng),
        in_specs=[
            pl.BlockSpec((None, S, POOL_GROUP), lambda b, g: (b, 0, COL_POOL // POOL_GROUP + g)),
            pl.BlockSpec((None, None, POOL_GROUP, POOL_GROUP), lambda b, g: (layer, g, 0, 0)),
            pl.BlockSpec((None, 1, POOL_GROUP), lambda b, g: (layer, 0, g)),
        ],
        out_specs=pl.BlockSpec((None, S, POOL_GROUP), lambda b, g: (b, 0, g)),
        out_shape=jax.ShapeDtypeStruct((B, S, POOL_WIDTH), BF16),
        scratch_shapes=[pltpu.VMEM((S + 2 * POOL_HALO, POOL_GROUP), BF16)],
        compiler_params=_cparams(("parallel", "arbitrary")),
        name="pool",
    )(proj, pool_w, pool_scale)


def _merge_kernel(y0_ref, y1_ref, y2_ref, g0_ref, g1_ref, g2_ref, w0_ref, w1_ref, w2_ref, o_ref):
    acc = g0_ref[...].astype(F32) * _dot(y0_ref[...], w0_ref[...])
    acc += g1_ref[...].astype(F32) * _dot(y1_ref[...], w1_ref[...])
    acc += g2_ref[...].astype(F32) * _dot(y2_ref[...], w2_ref[...])
    o_ref[...] = acc.astype(o_ref.dtype)


def _merge(y_ret, y_mla, y_pool, gates, w_branch, layer):
    T, W = y_ret.shape
    D = D_MODEL
    tm, tn = TM_MERGE, TN_MERGE
    ysp = pl.BlockSpec((tm, W), lambda i, j: (i, 0))

    def gate(b):
        return pl.BlockSpec((tm, tn), lambda i, j: (i, b * D // tn + j))

    def wsp(b):
        return pl.BlockSpec((None, None, W, tn), lambda i, j: (layer, b, 0, j))

    return pl.pallas_call(
        _merge_kernel,
        grid=(T // tm, D // tn),
        in_specs=[ysp, ysp, ysp, gate(0), gate(1), gate(2), wsp(0), wsp(1), wsp(2)],
        out_specs=pl.BlockSpec((tm, tn), lambda i, j: (i, j)),
        out_shape=jax.ShapeDtypeStruct((T, D), BF16),
        compiler_params=_cparams(("parallel", "arbitrary")),
        name="merge",
    )(y_ret, y_mla, y_pool, gates, gates, gates, w_branch, w_branch, w_branch)


def _outproj_router_kernel(m_ref, w_ref, x_ref, ln_ref, w_hi_ref, w_lo_ref, b_ref, o_ref, cls_ref):
    D = x_ref.shape[1]
    x = x_ref[...] + _dot(m_ref[...], w_ref[...])
    o_ref[:, 0:D] = x
    h = _rms(x, ln_ref[...])
    h_hi = h.astype(BF16)
    h_lo = (h - h_hi.astype(F32)).astype(BF16)
    lt = _dot_nt(w_hi_ref[...], h_hi) + _dot_nt(w_hi_ref[...], h_lo) + _dot_nt(w_lo_ref[...], h_hi)
    lt = lt + b_ref[...]
    grp = [lt[g:g + 1, :] for g in range(N_GROUPS)]
    gmax = functools.reduce(jnp.maximum, grp)
    gden = functools.reduce(lambda a, b: a + b, [jnp.exp(x - gmax) for x in grp])
    g_p = 1.0 / gden
    gidx = jnp.full(gmax.shape, N_GROUPS, jnp.int32)
    for g in reversed(range(N_GROUPS)):
        gidx = jnp.where(grp[g] == gmax, g, gidx)
    sel = []
    for j in range(EXPERTS_PER_GROUP):
        e = jnp.zeros_like(gmax)
        for g in range(N_GROUPS):
            r = N_GROUPS + g * EXPERTS_PER_GROUP + j
            e = jnp.where(gidx == g, lt[r:r + 1, :], e)
        sel.append(e)
    emax = functools.reduce(jnp.maximum, sel)
    ex = [jnp.exp(e - emax) for e in sel]
    eden = functools.reduce(lambda a, b: a + b, ex)
    pe = [e / eden for e in ex]

    def top1(vals):
        best = functools.reduce(jnp.maximum, vals)
        idx = jnp.full(best.shape, len(vals), jnp.int32)
        for j in reversed(range(len(vals))):
            idx = jnp.where(vals[j] == best, j, idx)
        return best, idx

    p1, i1 = top1(pe)
    p2, i2 = top1([jnp.where(i1 == j, -1.0, pe[j]) for j in range(EXPERTS_PER_GROUP)])
    den = p1 + p2
    w1 = g_p * (p1 / den)
    w2 = g_p * (p2 / den)
    lo = jnp.minimum(i1, i2)
    hi = jnp.maximum(i1, i2)
    pair = jnp.zeros_like(lo)
    for n, (a, b) in enumerate(PAIRS):
        pair = jnp.where((lo == a) & (hi == b), n, pair)
    cls_ref[...] = gidx * len(PAIRS) + pair
    first_is_lo = i1 < i2
    w_lo_exp = jnp.where(first_is_lo, w1, w2)
    w_hi_exp = jnp.where(first_is_lo, w2, w1)
    tm = x.shape[0]
    rid = lax.broadcasted_iota(jnp.int32, (ROW_EXTRA, tm), 0)
    tile = jnp.where(rid == 0, w_lo_exp, jnp.where(rid == 1, w_hi_exp, 0.0))
    o_ref[:, D:D + ROW_EXTRA] = tile.T


def _outproj_router(merged, w_out, x, ln, w_hi, w_lo, bias, layer):
    T, D = x.shape
    tm = TM_OUT
    n_log = N_GROUPS + N_EXPERTS
    rows = pl.BlockSpec((tm, D), lambda i: (i, 0))
    row = pl.BlockSpec((None, 1, tm), lambda i: (i, 0, 0))
    outs = pl.pallas_call(
        _outproj_router_kernel,
        grid=(T // tm,),
        in_specs=[
            rows,
            pl.BlockSpec((None, D, D), lambda i: (layer, 0, 0)),
            rows,
            pl.BlockSpec((None, 1, D), lambda i: (layer, 0, 0)),
            pl.BlockSpec((None, n_log, D), lambda i: (layer, 0, 0)),
            pl.BlockSpec((None, n_log, D), lambda i: (layer, 0, 0)),
            pl.BlockSpec((None, n_log, 1), lambda i: (layer, 0, 0)),
        ],
        out_specs=[pl.BlockSpec((tm, D + ROW_EXTRA), lambda i: (i, 0)), row],
        out_shape=[
            jax.ShapeDtypeStruct((T, D + ROW_EXTRA), F32),
            jax.ShapeDtypeStruct((T // tm, 1, tm), jnp.int32),
        ],
        compiler_params=_cparams(("parallel",)),
        name="outproj_router",
    )(merged, w_out, x, ln, w_hi, w_lo, bias)
    return outs[0], outs[1].reshape(T)


def _moe_kernel(elo_ref, ehi_ref, nused_ref, order_ref, base_ref, nval_ref,
                x_hbm, ln_ref, wg_lo, wu_lo, wd_lo, wg_hi, wu_hi, wd_hi,
                o_hbm, xbuf, obuf, gsem, ssem, *, n_tok):
    del elo_ref, ehi_ref
    bm = MOE_BLOCK
    D = obuf.shape[-1]
    i = pl.program_id(0)
    n_used = nused_ref[0]

    def start_gather(block, buf):
        b0, last = base_ref[block], nval_ref[block] - 1
        for r in range(bm):
            tok = order_ref[b0 + jnp.minimum(r, last)]
            pltpu.make_async_copy(x_hbm.at[tok], xbuf.at[buf, r], gsem.at[buf]).start()

    def wait_gather(buf):
        pltpu.make_async_copy(x_hbm.at[pl.ds(0, bm)], xbuf.at[buf], gsem.at[buf]).wait()

    def start_scatter(block, buf):
        blk = jnp.maximum(block, 0)
        b0 = base_ref[blk]
        nv = jnp.where(block >= 0, nval_ref[blk], 0)
        for r in range(bm):
            tok = order_ref[b0 + jnp.minimum(r, jnp.maximum(nv - 1, 0))]
            row = jnp.where(r < nv, tok, n_tok + r)
            pltpu.make_async_copy(obuf.at[buf, r], o_hbm.at[row], ssem).start()

    def wait_scatter():
        pltpu.make_async_copy(obuf.at[0], o_hbm.at[pl.ds(0, bm)], ssem).wait()

    @pl.when(i == 0)
    def _():
        obuf[...] = jnp.zeros(obuf.shape, F32)
        start_gather(0, 0)
        start_scatter(-1, 1)

    def step(cur):
        wait_gather(cur)
        wait_scatter()
        start_gather(jnp.minimum(i + 1, n_used - 1), 1 - cur)
        start_scatter(i - 1, 1 - cur)
        x = xbuf[cur, :, 0:D]
        w_lo = xbuf[cur, :, D:D + 1]
        w_hi = xbuf[cur, :, D + 1:D + 2]
        h = _rms(x, ln_ref[...]).astype(BF16)

        def ffn(wg, wu, wd):
            a = _dot(h, wg[...])
            hidden = (a * jax.nn.sigmoid(a)) * _dot(h, wu[...])
            return _dot(hidden.astype(BF16), wd[...])

        obuf[cur] = x + w_lo * ffn(wg_lo, wu_lo, wd_lo) + w_hi * ffn(wg_hi, wu_hi, wd_hi)

        @pl.when(i == n_used - 1)
        def _():
            wait_scatter()
            start_scatter(i, cur)
            wait_scatter()
            wait_gather(1 - cur)

    @pl.when((i < n_used) & (i % 2 == 0))
    def _():
        step(0)

    @pl.when((i < n_used) & (i % 2 == 1))
    def _():
        step(1)


def _moe(xw, elo, ehi, n_used, order, base, nval, ln, w_gate, w_up, w_down, layer):
    T, D = xw.shape[0], D_MODEL
    bm = MOE_BLOCK
    n_blocks = elo.shape[0]
    Hd = EXPERT_HIDDEN

    def wspec(shape, which):
        def imap(i, elo_r, ehi_r, *_):
            e = elo_r[i] if which == 0 else ehi_r[i]
            return (layer, e, 0, 0)
        return pl.BlockSpec((None, None) + shape, imap)

    grid_spec = pltpu.PrefetchScalarGridSpec(
        num_scalar_prefetch=6,
        grid=(n_blocks,),
        in_specs=[
            pl.BlockSpec(memory_space=pl.ANY),
            pl.BlockSpec((None, 1, D), lambda i, *_: (layer, 0, 0)),
            wspec((D, Hd), 0), wspec((D, Hd), 0), wspec((Hd, D), 0),
            wspec((D, Hd), 1), wspec((D, Hd), 1), wspec((Hd, D), 1),
        ],
        out_specs=pl.BlockSpec(memory_space=pl.ANY),
        scratch_shapes=[
            pltpu.VMEM((2, bm, D + ROW_EXTRA), F32), pltpu.VMEM((2, bm, D), F32),
            pltpu.SemaphoreType.DMA((2,)), pltpu.SemaphoreType.DMA,
        ],
    )
    return pl.pallas_call(
        functools.partial(_moe_kernel, n_tok=T),
        grid_spec=grid_spec,
        out_shape=jax.ShapeDtypeStruct((T + bm, D), F32),
        compiler_params=_cparams(("arbitrary",)),
        name="moe",
    )(elo, ehi, n_used, order, base, nval, xw, ln, w_gate, w_up, w_down, w_gate, w_up, w_down)


def _dispatch_plan(cls):
    T = cls.shape[0]
    bm = MOE_BLOCK
    n_blocks = (T + N_CLASSES * (bm - 1)) // bm
    order = jnp.argsort(cls).astype(jnp.int32)
    counts = jnp.sum(cls[None, :] == jnp.arange(N_CLASSES, dtype=jnp.int32)[:, None], axis=1, dtype=jnp.int32)
    padded = (counts + bm - 1) // bm * bm
    pends = jnp.cumsum(padded)
    pstarts = pends - padded
    starts = jnp.cumsum(counts) - counts
    n_used = pends[-1] // bm
    blk = jnp.arange(n_blocks, dtype=jnp.int32)
    bidx = jnp.minimum(blk, n_used - 1)
    bcls = jnp.minimum(jnp.sum(pends[None, :] <= (bidx * bm)[:, None], axis=1, dtype=jnp.int32), N_CLASSES - 1)
    onehot = bcls[:, None] == jnp.arange(N_CLASSES, dtype=jnp.int32)[None, :]

    def per_block(table):
        return jnp.sum(jnp.where(onehot, table[None, :], 0), axis=1, dtype=jnp.int32)

    off = bidx * bm - per_block(pstarts)
    nval = jnp.where(blk < n_used, jnp.clip(per_block(counts) - off, 0, bm), 0)
    base = per_block(starts) + off
    grp = bcls // len(PAIRS)
    pair = bcls % len(PAIRS)
    pair_lo, pair_hi = jnp.zeros_like(pair), jnp.zeros_like(pair)
    for n, (a, b) in enumerate(PAIRS):
        pair_lo = jnp.where(pair == n, a, pair_lo)
        pair_hi = jnp.where(pair == n, b, pair_hi)
    elo = grp * EXPERTS_PER_GROUP + pair_lo
    ehi = grp * EXPERTS_PER_GROUP + pair_hi
    return elo, ehi, n_used.reshape(1).astype(jnp.int32), order, base, nval


def _ple_kernel(x_ref, p_ref, ln_ref, wg_ref, wp_ref, fin_ref, o_ref, *, final_norm):
    x = x_ref[...]
    gate = jax.nn.sigmoid(_dot(_rms(x, ln_ref[...]).astype(BF16), wg_ref[...]))
    y = x + gate * _dot(p_ref[...].astype(BF16), wp_ref[...])
    o_ref[...] = _rms(y, fin_ref[...]) if final_norm else y


def _ple(x, p, ln, w_gate, w_proj, fin_gain, layer, final_norm):
    T, D = p.shape[1], x.shape[1]
    return pl.pallas_call(
        functools.partial(_ple_kernel, final_norm=final_norm),
        grid=(T // TM_PLE,),
        in_specs=[
            pl.BlockSpec((TM_PLE, D), lambda i: (i, 0)),
            pl.BlockSpec((None, TM_PLE, PLE_DIM), lambda i: (layer, i, 0)),
            pl.BlockSpec((None, 1, D), lambda i: (layer, 0, 0)),
            pl.BlockSpec((None, D, D), lambda i: (layer, 0, 0)),
            pl.BlockSpec((None, PLE_DIM, D), lambda i: (layer, 0, 0)),
            pl.BlockSpec((1, D), lambda i: (0, 0)),
        ],
        out_specs=pl.BlockSpec((TM_PLE, D), lambda i: (i, 0)),
        out_shape=jax.ShapeDtypeStruct((T, D), F32),
        compiler_params=_cparams(("parallel",)),
        name="ple_final" if final_norm else "ple",
    )(x, p, ln, w_gate, w_proj, fin_gain)


def _rope_tables(seq, dim):
    inv = 1.0 / (ROPE_BASE ** (jnp.arange(0, dim, 2, dtype=F32) / dim))
    ang = jnp.arange(seq, dtype=F32)[:, None] * inv[None, :]
    return jnp.cos(ang), jnp.sin(ang)


def _prep_in_weights(w_in, b_gate):
    depth, D, _ = w_in.shape
    ret_w = 4 * RET_WIDTH
    mla_w = MLA_Q_LORA + MLA_KV_LORA + MLA_ROPE
    pad = COL_POOL - COL_KR - MLA_ROPE
    w = jnp.concatenate([
        w_in[..., :ret_w + mla_w],
        jnp.zeros((depth, D, pad), w_in.dtype),
        w_in[..., ret_w + mla_w:],
    ], axis=-1).astype(BF16)
    b = jnp.concatenate([jnp.zeros((depth, COL_GATE), F32), b_gate], axis=-1).reshape(depth, 1, IN_PAD_WIDTH)
    return w, b


def _prep_mla_weights(w_uq, w_ukv):
    depth = w_uq.shape[0]
    wq = w_uq.reshape(depth, MLA_Q_LORA, MLA_HEADS, MLA_NOPE + MLA_ROPE)
    wq = jnp.pad(wq, ((0, 0), (0, 0), (0, 0), (0, MLA_QK_PAD - MLA_NOPE - MLA_ROPE)))
    wq = wq.reshape(depth, MLA_Q_LORA, MLA_HEADS * MLA_QK_PAD).astype(BF16)
    wkv = w_ukv.reshape(depth, MLA_KV_LORA, MLA_HEADS, MLA_NOPE + MLA_V)
    wkv = jnp.concatenate([
        wkv[..., :MLA_NOPE].reshape(depth, MLA_KV_LORA, MLA_HEADS * MLA_NOPE),
        wkv[..., MLA_NOPE:].reshape(depth, MLA_KV_LORA, MLA_HEADS * MLA_V),
    ], axis=-1).astype(BF16)
    return wq, wkv


def _split_bf16(w):
    hi = w.astype(BF16)
    lo = (w - hi.astype(F32)).astype(BF16)
    return hi, lo


def kernel(x, p, ln_mix, w_in, b_gate, ret_decay_logit, ret_norm, mla_q_norm, mla_w_uq, mla_kv_norm, mla_w_ukv, pool_w, pool_scale, w_branch, w_out, ln_moe, w_grp, b_grp, w_rt, b_rt, w_exp_gate, w_exp_up, w_exp_down, ln_ple, w_ple_gate, w_ple_proj, ln_final):
    B, S, D = x.shape
    depth = p.shape[0]
    T = B * S
    assert D == D_MODEL and S % max(TQ, TK, POOL_ROWS, TM_UP) == 0
    assert all(T % t == 0 for t in (TM_IN, TM_MERGE, TM_OUT, TM_PLE, MOE_BLOCK))

    cos_r, sin_r = _rope_tables(S, RET_HEAD_DIM)
    ret_cos = jnp.concatenate([cos_r, cos_r], axis=-1)
    ret_sin = jnp.concatenate([-sin_r, sin_r], axis=-1)
    cos_m, sin_m = _rope_tables(S, MLA_ROPE)
    z32 = jnp.zeros_like(sin_m)
    z64 = jnp.zeros((S, 128 - MLA_ROPE), F32)
    mla_cos = jnp.concatenate([cos_m, cos_m, z64], axis=-1)
    mla_sina = jnp.concatenate([z32, sin_m, z64], axis=-1)
    mla_sinb = jnp.concatenate([-sin_m, z32, z64], axis=-1)

    w_in_p, b_in = _prep_in_weights(w_in, b_gate)
    wq, wkv = _prep_mla_weights(mla_w_uq, mla_w_ukv)
    ln_mix3 = ln_mix.reshape(depth, 1, D)
    ln_moe3 = ln_moe.reshape(depth, 1, D)
    ln_ple3 = ln_ple.reshape(depth, 1, D)
    qn3 = mla_q_norm.reshape(depth, 1, MLA_Q_LORA)
    kvn3 = mla_kv_norm.reshape(depth, 1, MLA_KV_LORA)
    decay4 = jnp.broadcast_to(ret_decay_logit.transpose(0, 2, 1)[..., None], (depth, RET_HEADS, 2, RET_HEAD_DIM))
    gain4 = ret_norm.reshape(depth, RET_HEADS, 1, RET_HEAD_DIM)
    pool_w_b = pool_w.astype(BF16)
    pool_sc3 = pool_scale.reshape(depth, 1, POOL_WIDTH)
    w_branch_b = w_branch.astype(BF16)
    w_out_b = w_out.astype(BF16)
    w_router = jnp.concatenate([w_grp, w_rt], axis=-1).transpose(0, 2, 1)
    w_router_hi, w_router_lo = _split_bf16(w_router)
    b_router = jnp.concatenate([b_grp, b_rt], axis=-1).reshape(depth, N_GROUPS + N_EXPERTS, 1)
    w_eg = w_exp_gate.astype(BF16)
    w_eu = w_exp_up.astype(BF16)
    w_ed = w_exp_down.astype(BF16)
    w_pg = w_ple_gate.astype(BF16)
    w_pp = w_ple_proj.astype(BF16)
    p3 = p.reshape(depth, T, PLE_DIM)

    xt = x.reshape(T, D)
    for layer in range(depth):
        proj = _inproj(xt, ln_mix3, w_in_p, b_in, layer, 0, COL_GATE, False)
        gates = _inproj(xt, ln_mix3, w_in_p, b_in, layer, COL_GATE, N_BRANCH * D, True)
        proj3 = proj.reshape(B, S, COL_GATE)
        y_ret = _retention(proj3, ret_cos, ret_sin, decay4, gain4, layer)
        q, k, vt = _mla_up(proj, qn3, kvn3, wq, wkv, mla_cos, mla_sina, mla_sinb, layer, S)
        y_mla = _attention(q.reshape(B, S, -1), k.reshape(B, S, -1), vt)
        y_pool = _pool(proj3, pool_w_b, pool_sc3, layer)
        merged = _merge(y_ret.reshape(T, -1), y_mla.reshape(T, -1), y_pool.reshape(T, -1), gates, w_branch_b, layer)
        xw, cls = _outproj_router(merged, w_out_b, xt, ln_moe3, w_router_hi, w_router_lo, b_router, layer)
        elo, ehi, n_used, order, base, nval = _dispatch_plan(cls)
        xt = _moe(xw, elo, ehi, n_used, order, base, nval, ln_moe3, w_eg, w_eu, w_ed, layer)
        xt = _ple(xt, p3, ln_ple3, w_pg, w_pp, ln_final.reshape(1, D), layer, layer == depth - 1)
    return xt.reshape(B, S, D)
```

```python
import functools
import math

import jax
import jax.numpy as jnp
from jax import lax
from jax.experimental import pallas as pl
from jax.experimental.pallas import tpu as pltpu

F32 = jnp.float32
BF16 = jnp.bfloat16

D_MODEL = 2048
RET_HEADS = 8
RET_HEAD_DIM = 128
RET_WIDTH = RET_HEADS * RET_HEAD_DIM
RET_CHUNK = 128
MLA_HEADS = 8
MLA_Q_LORA = 512
MLA_KV_LORA = 256
MLA_NOPE = 128
MLA_ROPE = 64
MLA_V = 128
MLA_QK_PAD = 256
POOL_WINDOWS = (2, 4, 8, 16)
POOL_GROUP = 256
POOL_WIDTH = len(POOL_WINDOWS) * POOL_GROUP
N_BRANCH = 3
N_GROUPS = 8
EXPERTS_PER_GROUP = 4
N_EXPERTS = N_GROUPS * EXPERTS_PER_GROUP
EXPERT_HIDDEN = 512
PAIRS = ((0, 1), (0, 2), (0, 3), (1, 2), (1, 3), (2, 3))
N_CLASSES = N_GROUPS * len(PAIRS)
PLE_DIM = 256
ROPE_BASE = 10000.0
NORM_EPS = 1e-6

COL_RET = 0
COL_CQ = 4096
COL_CKV = COL_CQ + MLA_Q_LORA
COL_KR = COL_CKV + MLA_KV_LORA
COL_POOL = 5120
COL_GATE = 6144
IN_PAD_WIDTH = COL_GATE + N_BRANCH * D_MODEL

V7X_VMEM_LIMIT = 56 * 1024 * 1024

TM_IN, TN_IN = 1024, 2048
TM_MERGE, TN_MERGE = 512, 2048
TM_OUT = 512
TQ = 2048
TK = 512
ATTN_CHUNKS_PER_BODY = 2
ATTN_STRIP = 256
TM_UP = 512
MLA_VT_ROWS = MLA_V + 16
RET_UNROLL = 32
POOL_ROWS = 512
POOL_HALO = 128
MOE_BLOCK = 256
ROW_EXTRA = 128
TM_PLE = 512


def _cparams(sem):
    return pltpu.CompilerParams(dimension_semantics=sem, vmem_limit_bytes=V7X_VMEM_LIMIT)


def _rms(x, g):
    return x * lax.rsqrt(jnp.mean(x * x, axis=-1, keepdims=True) + NORM_EPS) * g


def _dot(a, b):
    return jnp.dot(a, b, preferred_element_type=F32)


def _dot_nt(a, b):
    return lax.dot_general(a, b, (((1,), (1,)), ((), ())), preferred_element_type=F32)


def _inproj_kernel(x_ref, g_ref, w_ref, b_ref, o_ref, h_scr, *, gated):
    @pl.when(pl.program_id(1) == 0)
    def _():
        h_scr[...] = _rms(x_ref[...], g_ref[...]).astype(BF16)

    acc = _dot(h_scr[...], w_ref[...])
    if gated:
        acc = jax.nn.sigmoid(acc + b_ref[...])
    o_ref[...] = acc.astype(o_ref.dtype)


def _inproj(x, ln, w, b, layer, col0, ncols, gated):
    T, D = x.shape
    tm, tn = TM_IN, TN_IN
    assert col0 % tn == 0 and ncols % tn == 0
    j0 = col0 // tn
    return pl.pallas_call(
        functools.partial(_inproj_kernel, gated=gated),
        grid=(T // tm, ncols // tn),
        in_specs=[
            pl.BlockSpec((tm, D), lambda i, j: (i, 0)),
            pl.BlockSpec((None, 1, D), lambda i, j: (layer, 0, 0)),
            pl.BlockSpec((None, D, tn), lambda i, j: (layer, 0, j0 + j)),
            pl.BlockSpec((None, 1, tn), lambda i, j: (layer, 0, j0 + j)),
        ],
        out_specs=pl.BlockSpec((tm, tn), lambda i, j: (i, j)),
        out_shape=jax.ShapeDtypeStruct((T, ncols), BF16),
        scratch_shapes=[pltpu.VMEM((tm, D), BF16)],
        compiler_params=_cparams(("parallel", "arbitrary")),
        name="inproj_gate" if gated else "inproj",
    )(x, ln, w, b)


def _ret_kernel(q_ref, k_ref, v_ref, g_ref, cos_ref, sin_ref, lg_ref, gain_ref, o_ref,
                acc_scr, qr_scr, kr_scr, *, seq):
    C = RET_CHUNK
    n_chunks = seq // C
    lg = jax.nn.log_sigmoid(lg_ref[...])
    lgf, lgb = lg[0:1, :], lg[1:2, :]
    ii = lax.broadcasted_iota(jnp.int32, (C, C), 0).astype(F32)
    ll = lax.broadcasted_iota(jnp.int32, (C, C), 1).astype(F32)
    diff = ii - ll
    decay = jnp.where(diff >= 0, jnp.exp(lgf * jnp.maximum(diff, 0.0)),
                      jnp.exp(lgb * jnp.maximum(-diff, 0.0)))
    xi_f = jnp.exp(lgf * (ii + 1.0))
    xi_b = jnp.exp(lgb * (C - ii))
    zeta_f = jnp.exp(lgf * (C - 1.0 - ii))
    zeta_b = jnp.exp(lgb * ii)
    cdec_f = jnp.exp(lgf * C)
    cdec_b = jnp.exp(lgb * C)
    k_scale = RET_HEAD_DIM ** -0.5

    def rope(x, rows):
        return x * cos_ref[rows, :] + pltpu.roll(x, RET_HEAD_DIM // 2, 1) * sin_ref[rows, :]

    def fwd(c, state):
        rows = pl.ds(pl.multiple_of(c * C, C), C)
        q = rope(q_ref[rows, :].astype(F32), rows)
        k = rope(k_ref[rows, :].astype(F32), rows) * k_scale
        qb, kb = q.astype(BF16), k.astype(BF16)
        qr_scr[rows, :] = qb
        kr_scr[rows, :] = kb
        v = v_ref[rows, :]
        s = _dot_nt(qb, kb) * decay
        inner = _dot(s.astype(BF16), v)
        cross = _dot(qb, state.astype(BF16)) * xi_f
        acc_scr[rows, :] = inner + cross
        kv = _dot((k * zeta_f).T.astype(BF16), v)
        return state * cdec_f + kv

    lax.fori_loop(0, n_chunks, fwd, jnp.zeros((RET_HEAD_DIM, RET_HEAD_DIM), F32), unroll=RET_UNROLL)

    def bwd(t, state):
        c = n_chunks - 1 - t
        rows = pl.ds(pl.multiple_of(c * C, C), C)
        qb = qr_scr[rows, :]
        v = v_ref[rows, :]
        y = acc_scr[rows, :] + _dot(qb, state.astype(BF16)) * xi_b
        yn = _rms(y, gain_ref[...])
        g = g_ref[rows, :].astype(F32)
        o_ref[rows, :] = (yn * (g * jax.nn.sigmoid(g))).astype(o_ref.dtype)
        kv = _dot((kr_scr[rows, :].astype(F32) * zeta_b).T.astype(BF16), v)
        return state * cdec_b + kv

    lax.fori_loop(0, n_chunks, bwd, jnp.zeros((RET_HEAD_DIM, RET_HEAD_DIM), F32), unroll=RET_UNROLL)


def _retention(proj, cos, sin, decay_logit, gain, layer):
    B, S, _ = proj.shape
    hd = RET_HEAD_DIM
    nh = RET_HEADS

    def col(part):
        return pl.BlockSpec((None, S, hd), lambda b, h: (b, 0, COL_RET // hd + part * nh + h))

    return pl.pallas_call(
        functools.partial(_ret_kernel, seq=S),
        grid=(B, nh),
        in_specs=[
            col(0), col(1), col(2), col(3),
            pl.BlockSpec((S, hd), lambda b, h: (0, 0)),
            pl.BlockSpec((S, hd), lambda b, h: (0, 0)),
            pl.BlockSpec((None, None, 2, hd), lambda b, h: (layer, h, 0, 0)),
            pl.BlockSpec((None, None, 1, hd), lambda b, h: (layer, h, 0, 0)),
        ],
        out_specs=pl.BlockSpec((None, S, hd), lambda b, h: (b, 0, h)),
        out_shape=jax.ShapeDtypeStruct((B, S, RET_WIDTH), BF16),
        scratch_shapes=[pltpu.VMEM((S, hd), F32), pltpu.VMEM((S, hd), BF16), pltpu.VMEM((S, hd), BF16)],
        compiler_params=_cparams(("parallel", "arbitrary")),
        name="retention",
    )(proj, proj, proj, proj, cos, sin, decay_logit, gain)


def _mla_up_kernel(cq_ref, ckv_ref, kr_ref, qn_ref, kvn_ref, wq_ref, wkv_ref, cos_ref, sina_ref, sinb_ref,
                   q_out, k_out, vt_out):
    scale = (MLA_NOPE + MLA_ROPE) ** -0.5 * math.log2(math.e)
    cosp, sina, sinb = cos_ref[...], sina_ref[...], sinb_ref[...]

    def rope(x):
        return x * cosp + pltpu.roll(x, MLA_ROPE // 2, 1) * sina + pltpu.roll(x, 128 - MLA_ROPE // 2, 1) * sinb

    q = _dot(_rms(cq_ref[...].astype(F32), qn_ref[...]).astype(BF16), wq_ref[...])
    kv = _dot(_rms(ckv_ref[...].astype(F32), kvn_ref[...]).astype(BF16), wkv_ref[...])
    k_rope = rope(kr_ref[...].astype(F32)).astype(BF16)
    for h in range(MLA_HEADS):
        lo = h * MLA_QK_PAD
        q_out[:, lo:lo + MLA_NOPE] = (q[:, lo:lo + MLA_NOPE] * scale).astype(BF16)
        q_out[:, lo + MLA_NOPE:lo + MLA_QK_PAD] = (rope(q[:, lo + MLA_NOPE:lo + MLA_QK_PAD]) * scale).astype(BF16)
        k_out[:, lo:lo + MLA_NOPE] = kv[:, h * MLA_NOPE:(h + 1) * MLA_NOPE].astype(BF16)
        k_out[:, lo + MLA_NOPE:lo + MLA_QK_PAD] = k_rope
    v_t = kv[:, MLA_HEADS * MLA_NOPE:].T
    ones = jnp.ones((MLA_VT_ROWS - MLA_V, v_t.shape[1]), BF16)
    for h in range(MLA_HEADS):
        lo = h * MLA_VT_ROWS
        vt_out[lo:lo + MLA_V, :] = v_t[h * MLA_V:(h + 1) * MLA_V, :].astype(BF16)
        vt_out[lo + MLA_V:lo + MLA_VT_ROWS, :] = ones


def _mla_up(proj2d, qn, kvn, wq, wkv, cosp, sina, sinb, layer, seq):
    T = proj2d.shape[0]
    tm = TM_UP
    n_s = seq // tm
    qk_w = MLA_HEADS * MLA_QK_PAD
    kv_w = MLA_HEADS * (MLA_NOPE + MLA_V)
    tab = pl.BlockSpec((tm, 128), lambda i: (i % n_s, 0))
    return pl.pallas_call(
        _mla_up_kernel,
        grid=(T // tm,),
        in_specs=[
            pl.BlockSpec((tm, MLA_Q_LORA), lambda i: (i, COL_CQ // MLA_Q_LORA)),
            pl.BlockSpec((tm, MLA_KV_LORA), lambda i: (i, COL_CKV // MLA_KV_LORA)),
            pl.BlockSpec((tm, 128), lambda i: (i, COL_KR // 128)),
            pl.BlockSpec((None, 1, MLA_Q_LORA), lambda i: (layer, 0, 0)),
            pl.BlockSpec((None, 1, MLA_KV_LORA), lambda i: (layer, 0, 0)),
            pl.BlockSpec((None, MLA_Q_LORA, qk_w), lambda i: (layer, 0, 0)),
            pl.BlockSpec((None, MLA_KV_LORA, kv_w), lambda i: (layer, 0, 0)),
            tab, tab, tab,
        ],
        out_specs=[
            pl.BlockSpec((tm, qk_w), lambda i: (i, 0)),
            pl.BlockSpec((tm, qk_w), lambda i: (i, 0)),
            pl.BlockSpec((None, MLA_HEADS * MLA_VT_ROWS, tm), lambda i: (i // n_s, 0, i % n_s)),
        ],
        out_shape=[
            jax.ShapeDtypeStruct((T, qk_w), BF16),
            jax.ShapeDtypeStruct((T, qk_w), BF16),
            jax.ShapeDtypeStruct((T // seq, MLA_HEADS * MLA_VT_ROWS, seq), BF16),
        ],
        compiler_params=_cparams(("parallel",)),
        name="mla_up",
    )(proj2d, proj2d, proj2d, qn, kvn, wq, wkv, cosp, sina, sinb)


def _attn_kernel(q_ref, k_ref, vt_ref, o_ref, acc_scr, st_a, st_b, *, seq):
    n = seq // TK
    U = ATTN_CHUNKS_PER_BODY
    q = q_ref[...]
    acc_scr[...] = jnp.zeros(acc_scr.shape, F32)
    bufs = (st_a, st_b)

    def scores(c, dst):
        r0 = pl.multiple_of(c * TK, TK)
        dst[...] = _dot_nt(k_ref[pl.ds(r0, TK), :], q)

    def absorb(c, src, m):
        r0 = pl.multiple_of(c * TK, TK)
        m_new = jnp.maximum(m, jnp.max(src[...], axis=0, keepdims=True))
        alpha = jnp.exp2(m - m_new)
        for qb in range(TQ // ATTN_STRIP):
            qs = slice(qb * ATTN_STRIP, (qb + 1) * ATTN_STRIP)
            a = alpha[:, qs] * acc_scr[:, qs]
            for kb in range(TK // ATTN_STRIP):
                ks = slice(kb * ATTN_STRIP, (kb + 1) * ATTN_STRIP)
                pt = jnp.exp2(src[ks, qs] - m_new[:, qs]).astype(BF16)
                a = a + _dot(vt_ref[:, pl.ds(r0 + kb * ATTN_STRIP, ATTN_STRIP)], pt)
            acc_scr[:, qs] = a
        return m_new

    scores(0, st_a)

    def body(jj, m):
        for u in range(U):
            scores(U * jj + u + 1, bufs[(u + 1) % 2])
            m = absorb(U * jj + u, bufs[u % 2], m)
        return m

    m = lax.fori_loop(0, n // U - 1, body, jnp.full((1, TQ), -jnp.inf, F32))
    for u in range(U):
        c = n - U + u
        if u + 1 < U:
            scores(c + 1, bufs[(u + 1) % 2])
        m = absorb(c, bufs[u % 2], m)
    acc = acc_scr[...]
    out_t = acc[0:MLA_V, :] / acc[MLA_V:MLA_V + 1, :]
    o_ref[...] = out_t.T.astype(o_ref.dtype)


def _attention(q, k, vt):
    B, S, _ = q.shape
    assert ATTN_CHUNKS_PER_BODY % 2 == 0 and (S // TK) % ATTN_CHUNKS_PER_BODY == 0
    return pl.pallas_call(
        functools.partial(_attn_kernel, seq=S),
        grid=(B, MLA_HEADS, S // TQ),
        in_specs=[
            pl.BlockSpec((None, TQ, MLA_QK_PAD), lambda b, h, i: (b, i, h)),
            pl.BlockSpec((None, S, MLA_QK_PAD), lambda b, h, i: (b, 0, h)),
            pl.BlockSpec((None, MLA_VT_ROWS, S), lambda b, h, i: (b, h, 0)),
        ],
        out_specs=pl.BlockSpec((None, TQ, MLA_V), lambda b, h, i: (b, i, h)),
        out_shape=jax.ShapeDtypeStruct((B, S, MLA_HEADS * MLA_V), BF16),
        scratch_shapes=[pltpu.VMEM((MLA_VT_ROWS, TQ), F32), pltpu.VMEM((TK, TQ), F32), pltpu.VMEM((TK, TQ), F32)],
        compiler_params=_cparams(("parallel", "parallel", "arbitrary")),
        name="attention",
    )(q, k, vt)


def _pool_kernel(u_ref, w_ref, sc_ref, o_ref, pad_scr, *, seq):
    gi = pl.program_id(1)
    half = jnp.int32(0)
    for k, w in enumerate(POOL_WINDOWS):
        half = jnp.where(gi == k, w // 2, half)
    R, H = POOL_ROWS, POOL_HALO
    pad_scr[0:H, :] = jnp.zeros((H, POOL_GROUP), BF16)
    pad_scr[H + seq:H + seq + H, :] = jnp.zeros((H, POOL_GROUP), BF16)
    pad_scr[H:H + seq, :] = u_ref[...]
    ii = lax.broadcasted_iota(jnp.int32, (R, R + 2 * H), 0)
    jj = lax.broadcasted_iota(jnp.int32, (R, R + 2 * H), 1)
    band = ((jj >= ii + H - half) & (jj < ii + H + half)).astype(BF16)
    rr = lax.broadcasted_iota(jnp.int32, (R, 1), 0)

    def body(t, carry):
        r0 = pl.multiple_of(t * R, R)
        total = _dot(band, pad_scr[pl.ds(r0, R + 2 * H), :])
        pos = r0 + rr
        cnt = (jnp.minimum(pos + half, seq) - jnp.maximum(pos - half, 0)).astype(F32)
        mixed = total / cnt - u_ref[pl.ds(r0, R), :].astype(F32)
        y = _dot(mixed.astype(BF16), w_ref[...]) * sc_ref[...]
        o_ref[pl.ds(r0, R), :] = y.astype(o_ref.dtype)
        return carry

    lax.fori_loop(0, seq // R, body, 0)


def _pool(proj, pool_w, pool_scale, layer):
    B, S, _ = proj.shape
    ng = len(POOL_WINDOWS)
    return pl.pallas_call(
        functools.partial(_pool_kernel, seq=S),
        grid=(B, ng),
        in_specs=[
            pl.BlockSpec((None, S, POOL_GROUP), lambda b, g: (b, 0, COL_POOL // POOL_GROUP + g)),
            pl.BlockSpec((None, None, POOL_GROUP, POOL_GROUP), lambda b, g: (layer, g, 0, 0)),
            pl.BlockSpec((None, 1, POOL_GROUP), lambda b, g: (layer, 0, g)),
        ],
        out_specs=pl.BlockSpec((None, S, POOL_GROUP), lambda b, g: (b, 0, g)),
        out_shape=jax.ShapeDtypeStruct((B, S, POOL_WIDTH), BF16),
        scratch_shapes=[pltpu.VMEM((S + 2 * POOL_HALO, POOL_GROUP), BF16)],
        compiler_params=_cparams(("parallel", "arbitrary")),
        name="pool",
    )(proj, pool_w, pool_scale)


def _merge_kernel(y0_ref, y1_ref, y2_ref, g0_ref, g1_ref, g2_ref, w0_ref, w1_ref, w2_ref, o_ref):
    acc = g0_ref[...].astype(F32) * _dot(y0_ref[...], w0_ref[...])
    acc += g1_ref[...].astype(F32) * _dot(y1_ref[...], w1_ref[...])
    acc += g2_ref[...].astype(F32) * _dot(y2_ref[...], w2_ref[...])
    o_ref[...] = acc.astype(o_ref.dtype)


def _merge(y_ret, y_mla, y_pool, gates, w_branch, layer):
    T, W = y_ret.shape
    D = D_MODEL
    tm, tn = TM_MERGE, TN_MERGE
    ysp = pl.BlockSpec((tm, W), lambda i, j: (i, 0))

    def gate(b):
        return pl.BlockSpec((tm, tn), lambda i, j: (i, b * D // tn + j))

    def wsp(b):
        return pl.BlockSpec((None, None, W, tn), lambda i, j: (layer, b, 0, j))

    return pl.pallas_call(
        _merge_kernel,
        grid=(T // tm, D // tn),
        in_specs=[ysp, ysp, ysp, gate(0), gate(1), gate(2), wsp(0), wsp(1), wsp(2)],
        out_specs=pl.BlockSpec((tm, tn), lambda i, j: (i, j)),
        out_shape=jax.ShapeDtypeStruct((T, D), BF16),
        compiler_params=_cparams(("parallel", "arbitrary")),
        name="merge",
    )(y_ret, y_mla, y_pool, gates, gates, gates, w_branch, w_branch, w_branch)


def _outproj_router_kernel(m_ref, w_ref, x_ref, ln_ref, w_hi_ref, w_lo_ref, b_ref, o_ref, cls_ref):
    D = x_ref.shape[1]
    x = x_ref[...] + _dot(m_ref[...], w_ref[...])
    o_ref[:, 0:D] = x
    h = _rms(x, ln_ref[...])
    h_hi = h.astype(BF16)
    h_lo = (h - h_hi.astype(F32)).astype(BF16)
    lt = _dot_nt(w_hi_ref[...], h_hi) + _dot_nt(w_hi_ref[...], h_lo) + _dot_nt(w_lo_ref[...], h_hi)
    lt = lt + b_ref[...]
    grp = [lt[g:g + 1, :] for g in range(N_GROUPS)]
    gmax = functools.reduce(jnp.maximum, grp)
    gden = functools.reduce(lambda a, b: a + b, [jnp.exp(x - gmax) for x in grp])
    g_p = 1.0 / gden
    gidx = jnp.full(gmax.shape, N_GROUPS, jnp.int32)
    for g in reversed(range(N_GROUPS)):
        gidx = jnp.where(grp[g] == gmax, g, gidx)
    sel = []
    for j in range(EXPERTS_PER_GROUP):
        e = jnp.zeros_like(gmax)
        for g in range(N_GROUPS):
            r = N_GROUPS + g * EXPERTS_PER_GROUP + j
            e = jnp.where(gidx == g, lt[r:r + 1, :], e)
        sel.append(e)
    emax = functools.reduce(jnp.maximum, sel)
    ex = [jnp.exp(e - emax) for e in sel]
    eden = functools.reduce(lambda a, b: a + b, ex)
    pe = [e / eden for e in ex]

    def top1(vals):
        best = functools.reduce(jnp.maximum, vals)
        idx = jnp.full(best.shape, len(vals), jnp.int32)
        for j in reversed(range(len(vals))):
            idx = jnp.where(vals[j] == best, j, idx)
        return best, idx

    p1, i1 = top1(pe)
    p2, i2 = top1([jnp.where(i1 == j, -1.0, pe[j]) for j in range(EXPERTS_PER_GROUP)])
    den = p1 + p2
    w1 = g_p * (p1 / den)
    w2 = g_p * (p2 / den)
    lo = jnp.minimum(i1, i2)
    hi = jnp.maximum(i1, i2)
    pair = jnp.zeros_like(lo)
    for n, (a, b) in enumerate(PAIRS):
        pair = jnp.where((lo == a) & (hi == b), n, pair)
    cls_ref[...] = gidx * len(PAIRS) + pair
    first_is_lo = i1 < i2
    w_lo_exp = jnp.where(first_is_lo, w1, w2)
    w_hi_exp = jnp.where(first_is_lo, w2, w1)
    tm = x.shape[0]
    rid = lax.broadcasted_iota(jnp.int32, (ROW_EXTRA, tm), 0)
    tile = jnp.where(rid == 0, w_lo_exp, jnp.where(rid == 1, w_hi_exp, 0.0))
    o_ref[:, D:D + ROW_EXTRA] = tile.T


def _outproj_router(merged, w_out, x, ln, w_hi, w_lo, bias, layer):
    T, D = x.shape
    tm = TM_OUT
    n_log = N_GROUPS + N_EXPERTS
    rows = pl.BlockSpec((tm, D), lambda i: (i, 0))
    row = pl.BlockSpec((None, 1, tm), lambda i: (i, 0, 0))
    outs = pl.pallas_call(
        _outproj_router_kernel,
        grid=(T // tm,),
        in_specs=[
            rows,
            pl.BlockSpec((None, D, D), lambda i: (layer, 0, 0)),
            rows,
            pl.BlockSpec((None, 1, D), lambda i: (layer, 0, 0)),
            pl.BlockSpec((None, n_log, D), lambda i: (layer, 0, 0)),
            pl.BlockSpec((None, n_log, D), lambda i: (layer, 0, 0)),
            pl.BlockSpec((None, n_log, 1), lambda i: (layer, 0, 0)),
        ],
        out_specs=[pl.BlockSpec((tm, D + ROW_EXTRA), lambda i: (i, 0)), row],
        out_shape=[
            jax.ShapeDtypeStruct((T, D + ROW_EXTRA), F32),
            jax.ShapeDtypeStruct((T // tm, 1, tm), jnp.int32),
        ],
        compiler_params=_cparams(("parallel",)),
        name="outproj_router",
    )(merged, w_out, x, ln, w_hi, w_lo, bias)
    return outs[0], outs[1].reshape(T)


def _moe_kernel(elo_ref, ehi_ref, nused_ref, order_ref, base_ref, nval_ref,
                x_hbm, ln_ref, wg_lo, wu_lo, wd_lo, wg_hi, wu_hi, wd_hi,
                o_hbm, xbuf, obuf, gsem, ssem, *, n_tok):
    del elo_ref, ehi_ref
    bm = MOE_BLOCK
    D = obuf.shape[-1]
    i = pl.program_id(0)
    n_used = nused_ref[0]

    def start_gather(block, buf):
        b0, last = base_ref[block], nval_ref[block] - 1
        for r in range(bm):
            tok = order_ref[b0 + jnp.minimum(r, last)]
            pltpu.make_async_copy(x_hbm.at[tok], xbuf.at[buf, r], gsem.at[buf]).start()

    def wait_gather(buf):
        pltpu.make_async_copy(x_hbm.at[pl.ds(0, bm)], xbuf.at[buf], gsem.at[buf]).wait()

    def start_scatter(block, buf):
        blk = jnp.maximum(block, 0)
        b0 = base_ref[blk]
        nv = jnp.where(block >= 0, nval_ref[blk], 0)
        for r in range(bm):
            tok = order_ref[b0 + jnp.minimum(r, jnp.maximum(nv - 1, 0))]
            row = jnp.where(r < nv, tok, n_tok + r)
            pltpu.make_async_copy(obuf.at[buf, r], o_hbm.at[row], ssem).start()

    def wait_scatter():
        pltpu.make_async_copy(obuf.at[0], o_hbm.at[pl.ds(0, bm)], ssem).wait()

    @pl.when(i == 0)
    def _():
        obuf[...] = jnp.zeros(obuf.shape, F32)
        start_gather(0, 0)
        start_scatter(-1, 1)

    def step(cur):
        wait_gather(cur)
        wait_scatter()
        start_gather(jnp.minimum(i + 1, n_used - 1), 1 - cur)
        start_scatter(i - 1, 1 - cur)
        x = xbuf[cur, :, 0:D]
        w_lo = xbuf[cur, :, D:D + 1]
        w_hi = xbuf[cur, :, D + 1:D + 2]
        h = _rms(x, ln_ref[...]).astype(BF16)

        def ffn(wg, wu, wd):
            a = _dot(h, wg[...])
            hidden = (a * jax.nn.sigmoid(a)) * _dot(h, wu[...])
            return _dot(hidden.astype(BF16), wd[...])

        obuf[cur] = x + w_lo * ffn(wg_lo, wu_lo, wd_lo) + w_hi * ffn(wg_hi, wu_hi, wd_hi)

        @pl.when(i == n_used - 1)
        def _():
            wait_scatter()
            start_scatter(i, cur)
            wait_scatter()
            wait_gather(1 - cur)

    @pl.when((i < n_used) & (i % 2 == 0))
    def _():
        step(0)

    @pl.when((i < n_used) & (i % 2 == 1))
    def _():
        step(1)


def _moe(xw, elo, ehi, n_used, order, base, nval, ln, w_gate, w_up, w_down, layer):
    T, D = xw.shape[0], D_MODEL
    bm = MOE_BLOCK
    n_blocks = elo.shape[0]
    Hd = EXPERT_HIDDEN

    def wspec(shape, which):
        def imap(i, elo_r, ehi_r, *_):
            e = elo_r[i] if which == 0 else ehi_r[i]
            return (layer, e, 0, 0)
        return pl.BlockSpec((None, None) + shape, imap)

    grid_spec = pltpu.PrefetchScalarGridSpec(
        num_scalar_prefetch=6,
        grid=(n_blocks,),
        in_specs=[
            pl.BlockSpec(memory_space=pl.ANY),
            pl.BlockSpec((None, 1, D), lambda i, *_: (layer, 0, 0)),
            wspec((D, Hd), 0), wspec((D, Hd), 0), wspec((Hd, D), 0),
            wspec((D, Hd), 1), wspec((D, Hd), 1), wspec((Hd, D), 1),
        ],
        out_specs=pl.BlockSpec(memory_space=pl.ANY),
        scratch_shapes=[
            pltpu.VMEM((2, bm, D + ROW_EXTRA), F32), pltpu.VMEM((2, bm, D), F32),
            pltpu.SemaphoreType.DMA((2,)), pltpu.SemaphoreType.DMA,
        ],
    )
    return pl.pallas_call(
        functools.partial(_moe_kernel, n_tok=T),
        grid_spec=grid_spec,
        out_shape=jax.ShapeDtypeStruct((T + bm, D), F32),
        compiler_params=_cparams(("arbitrary",)),
        name="moe",
    )(elo, ehi, n_used, order, base, nval, xw, ln, w_gate, w_up, w_down, w_gate, w_up, w_down)


def _dispatch_plan(cls):
    T = cls.shape[0]
    bm = MOE_BLOCK
    n_blocks = (T + N_CLASSES * (bm - 1)) // bm
    order = jnp.argsort(cls).astype(jnp.int32)
    counts = jnp.sum(cls[None, :] == jnp.arange(N_CLASSES, dtype=jnp.int32)[:, None], axis=1, dtype=jnp.int32)
    padded = (counts + bm - 1) // bm * bm
    pends = jnp.cumsum(padded)
    pstarts = pends - padded
    starts = jnp.cumsum(counts) - counts
    n_used = pends[-1] // bm
    blk = jnp.arange(n_blocks, dtype=jnp.int32)
    bidx = jnp.minimum(blk, n_used - 1)
    bcls = jnp.minimum(jnp.sum(pends[None, :] <= (bidx * bm)[:, None], axis=1, dtype=jnp.int32), N_CLASSES - 1)
    onehot = bcls[:, None] == jnp.arange(N_CLASSES, dtype=jnp.int32)[None, :]

    def per_block(table):
        return jnp.sum(jnp.where(onehot, table[None, :], 0), axis=1, dtype=jnp.int32)

    off = bidx * bm - per_block(pstarts)
    nval = jnp.where(blk < n_used, jnp.clip(per_block(counts) - off, 0, bm), 0)
    base = per_block(starts) + off
    grp = bcls // len(PAIRS)
    pair = bcls % len(PAIRS)
    pair_lo, pair_hi = jnp.zeros_like(pair), jnp.zeros_like(pair)
    for n, (a, b) in enumerate(PAIRS):
        pair_lo = jnp.where(pair == n, a, pair_lo)
        pair_hi = jnp.where(pair == n, b, pair_hi)
    elo = grp * EXPERTS_PER_GROUP + pair_lo
    ehi = grp * EXPERTS_PER_GROUP + pair_hi
    return elo, ehi, n_used.reshape(1).astype(jnp.int32), order, base, nval


def _ple_kernel(x_ref, p_ref, ln_ref, wg_ref, wp_ref, fin_ref, o_ref, *, final_norm):
    x = x_ref[...]
    gate = jax.nn.sigmoid(_dot(_rms(x, ln_ref[...]).astype(BF16), wg_ref[...]))
    y = x + gate * _dot(p_ref[...].astype(BF16), wp_ref[...])
    o_ref[...] = _rms(y, fin_ref[...]) if final_norm else y


def _ple(x, p, ln, w_gate, w_proj, fin_gain, layer, final_norm):
    T, D = p.shape[1], x.shape[1]
    return pl.pallas_call(
        functools.partial(_ple_kernel, final_norm=final_norm),
        grid=(T // TM_PLE,),
        in_specs=[
            pl.BlockSpec((TM_PLE, D), lambda i: (i, 0)),
            pl.BlockSpec((None, TM_PLE, PLE_DIM), lambda i: (layer, i, 0)),
            pl.BlockSpec((None, 1, D), lambda i: (layer, 0, 0)),
            pl.BlockSpec((None, D, D), lambda i: (layer, 0, 0)),
            pl.BlockSpec((None, PLE_DIM, D), lambda i: (layer, 0, 0)),
            pl.BlockSpec((1, D), lambda i: (0, 0)),
        ],
        out_specs=pl.BlockSpec((TM_PLE, D), lambda i: (i, 0)),
        out_shape=jax.ShapeDtypeStruct((T, D), F32),
        compiler_params=_cparams(("parallel",)),
        name="ple_final" if final_norm else "ple",
    )(x, p, ln, w_gate, w_proj, fin_gain)


def _rope_tables(seq, dim):
    inv = 1.0 / (ROPE_BASE ** (jnp.arange(0, dim, 2, dtype=F32) / dim))
    ang = jnp.arange(seq, dtype=F32)[:, None] * inv[None, :]
    return jnp.cos(ang), jnp.sin(ang)


def _prep_in_weights(w_in, b_gate):
    depth, D, _ = w_in.shape
    ret_w = 4 * RET_WIDTH
    mla_w = MLA_Q_LORA + MLA_KV_LORA + MLA_ROPE
    pad = COL_POOL - COL_KR - MLA_ROPE
    w = jnp.concatenate([
        w_in[..., :ret_w + mla_w],
        jnp.zeros((depth, D, pad), w_in.dtype),
        w_in[..., ret_w + mla_w:],
    ], axis=-1).astype(BF16)
    b = jnp.concatenate([jnp.zeros((depth, COL_GATE), F32), b_gate], axis=-1).reshape(depth, 1, IN_PAD_WIDTH)
    return w, b


def _prep_mla_weights(w_uq, w_ukv):
    depth = w_uq.shape[0]
    wq = w_uq.reshape(depth, MLA_Q_LORA, MLA_HEADS, MLA_NOPE + MLA_ROPE)
    wq = jnp.pad(wq, ((0, 0), (0, 0), (0, 0), (0, MLA_QK_PAD - MLA_NOPE - MLA_ROPE)))
    wq = wq.reshape(depth, MLA_Q_LORA, MLA_HEADS * MLA_QK_PAD).astype(BF16)
    wkv = w_ukv.reshape(depth, MLA_KV_LORA, MLA_HEADS, MLA_NOPE + MLA_V)
    wkv = jnp.concatenate([
        wkv[..., :MLA_NOPE].reshape(depth, MLA_KV_LORA, MLA_HEADS * MLA_NOPE),
        wkv[..., MLA_NOPE:].reshape(depth, MLA_KV_LORA, MLA_HEADS * MLA_V),
    ], axis=-1).astype(BF16)
    return wq, wkv


def _split_bf16(w):
    hi = w.astype(BF16)
    lo = (w - hi.astype(F32)).astype(BF16)
    return hi, lo


def kernel(x, p, ln_mix, w_in, b_gate, ret_decay_logit, ret_norm, mla_q_norm, mla_w_uq, mla_kv_norm, mla_w_ukv, pool_w, pool_scale, w_branch, w_out, ln_moe, w_grp, b_grp, w_rt, b_rt, w_exp_gate, w_exp_up, w_exp_down, ln_ple, w_ple_gate, w_ple_proj, ln_final):
    B, S, D = x.shape
    depth = p.shape[0]
    T = B * S
    assert D == D_MODEL and S % max(TQ, TK, POOL_ROWS, TM_UP) == 0
    assert all(T % t == 0 for t in (TM_IN, TM_MERGE, TM_OUT, TM_PLE, MOE_BLOCK))

    cos_r, sin_r = _rope_tables(S, RET_HEAD_DIM)
    ret_cos = jnp.concatenate([cos_r, cos_r], axis=-1)
    ret_sin = jnp.concatenate([-sin_r, sin_r], axis=-1)
    cos_m, sin_m = _rope_tables(S, MLA_ROPE)
    z32 = jnp.zeros_like(sin_m)
    z64 = jnp.zeros((S, 128 - MLA_ROPE), F32)
    mla_cos = jnp.concatenate([cos_m, cos_m, z64], axis=-1)
    mla_sina = jnp.concatenate([z32, sin_m, z64], axis=-1)
    mla_sinb = jnp.concatenate([-sin_m, z32, z64], axis=-1)

    w_in_p, b_in = _prep_in_weights(w_in, b_gate)
    wq, wkv = _prep_mla_weights(mla_w_uq, mla_w_ukv)
    ln_mix3 = ln_mix.reshape(depth, 1, D)
    ln_moe3 = ln_moe.reshape(depth, 1, D)
    ln_ple3 = ln_ple.reshape(depth, 1, D)
    qn3 = mla_q_norm.reshape(depth, 1, MLA_Q_LORA)
    kvn3 = mla_kv_norm.reshape(depth, 1, MLA_KV_LORA)
    decay4 = jnp.broadcast_to(ret_decay_logit.transpose(0, 2, 1)[..., None], (depth, RET_HEADS, 2, RET_HEAD_DIM))
    gain4 = ret_norm.reshape(depth, RET_HEADS, 1, RET_HEAD_DIM)
    pool_w_b = pool_w.astype(BF16)
    pool_sc3 = pool_scale.reshape(depth, 1, POOL_WIDTH)
    w_branch_b = w_branch.astype(BF16)
    w_out_b = w_out.astype(BF16)
    w_router = jnp.concatenate([w_grp, w_rt], axis=-1).transpose(0, 2, 1)
    w_router_hi, w_router_lo = _split_bf16(w_router)
    b_router = jnp.concatenate([b_grp, b_rt], axis=-1).reshape(depth, N_GROUPS + N_EXPERTS, 1)
    w_eg = w_exp_gate.astype(BF16)
    w_eu = w_exp_up.astype(BF16)
    w_ed = w_exp_down.astype(BF16)
    w_pg = w_ple_gate.astype(BF16)
    w_pp = w_ple_proj.astype(BF16)
    p3 = p.reshape(depth, T, PLE_DIM)

    xt = x.reshape(T, D)
    for layer in range(depth):
        proj = _inproj(xt, ln_mix3, w_in_p, b_in, layer, 0, COL_GATE, False)
        gates = _inproj(xt, ln_mix3, w_in_p, b_in, layer, COL_GATE, N_BRANCH * D, True)
        proj3 = proj.reshape(B, S, COL_GATE)
        y_ret = _retention(proj3, ret_cos, ret_sin, decay4, gain4, layer)
        q, k, vt = _mla_up(proj, qn3, kvn3, wq, wkv, mla_cos, mla_sina, mla_sinb, layer, S)
        y_mla = _attention(q.reshape(B, S, -1), k.reshape(B, S, -1), vt)
        y_pool = _pool(proj3, pool_w_b, pool_sc3, layer)
        merged = _merge(y_ret.reshape(T, -1), y_mla.reshape(T, -1), y_pool.reshape(T, -1), gates, w_branch_b, layer)
        xw, cls = _outproj_router(merged, w_out_b, xt, ln_moe3, w_router_hi, w_router_lo, b_router, layer)
        elo, ehi, n_used, order, base, nval = _dispatch_plan(cls)
        xt = _moe(xw, elo, ehi, n_used, order, base, nval, ln_moe3, w_eg, w_eu, w_ed, layer)
        xt = _ple(xt, p3, ln_ple3, w_pg, w_pp, ln_final.reshape(1, D), layer, layer == depth - 1)
    return xt.reshape(B, S, D)
```
